```python
import math
import jax, jax.numpy as jnp
from jax import lax
import numpy as np

D_MODEL = 1024
BATCH = 16
SEQ = 2048
DEPTH = 2

CHUNK = 64
PAST_CHUNKS = 8
BAND = (PAST_CHUNKS + 1) * CHUNK
PAD = PAST_CHUNKS * CHUNK
N_LAYERS_A = DEPTH // 2
N_LAYERS_B = DEPTH - N_LAYERS_A
HEAD_DIM = 64
A_HEADS = D_MODEL // HEAD_DIM
A_WIDTH = A_HEADS * HEAD_DIM
MAX_REL = 128
B_HEADS = D_MODEL // (2 * HEAD_DIM)
B_WIDTH = B_HEADS * 2 * HEAD_DIM
ROT_DIM = HEAD_DIM // 4
ROPE_THETA = 500000.0
Q_BLOCK = 128
RMS_EPS = 1e-6
NEG_INF = -1e30

kernel_name = "yoco_chunked_relpos_diff_attention_trunk"


def rmsnorm(t, g):
    tf = t.astype(jnp.float32)
    y = tf * lax.rsqrt(jnp.mean(tf * tf, axis=-1, keepdims=True) + RMS_EPS)
    return (y * g.astype(jnp.float32)).astype(t.dtype)


def partial_rope(t, positions):
    half = ROT_DIM // 2
    inv_freq = jnp.power(jnp.float32(ROPE_THETA), -jnp.arange(half, dtype=jnp.float32) * 2.0 / ROT_DIM)
    ang = positions.astype(jnp.float32)[..., None] * inv_freq
    ang = ang.reshape(ang.shape[:2] + (1,) * (t.ndim - 3) + (half,))
    cos = jnp.cos(ang).astype(t.dtype)
    sin = jnp.sin(ang).astype(t.dtype)
    t1 = t[..., :half]
    t2 = t[..., half:ROT_DIM]
    return jnp.concatenate([t1 * cos - t2 * sin, t2 * cos + t1 * sin, t[..., ROT_DIM:]], axis=-1)


def chunk_band_attention(q, k, v, positions, rel_bias):
    B, S, H, dh = q.shape
    n_chunks = S // CHUNK
    scale = 1.0 / math.sqrt(dh)
    qt = q.transpose(0, 2, 1, 3).reshape(B, H, n_chunks, CHUNK, dh).transpose(2, 0, 1, 3, 4)
    kp = jnp.pad(k.transpose(0, 2, 1, 3), ((0, 0), (0, 0), (PAD, 0), (0, 0)))
    vp = jnp.pad(v.transpose(0, 2, 1, 3), ((0, 0), (0, 0), (PAD, 0), (0, 0)))
    posp = jnp.pad(positions, ((0, 0), (PAD, 0)))

    def one_chunk(args):
        qc, c = args
        start = c * CHUNK
        kb = lax.dynamic_slice_in_dim(kp, start, BAND, axis=2)
        vb = lax.dynamic_slice_in_dim(vp, start, BAND, axis=2)
        kpos = lax.dynamic_slice_in_dim(posp, start, BAND, axis=1)
        qpos = lax.dynamic_slice_in_dim(positions, start, CHUNK, axis=1)
        valid = (start + jnp.arange(BAND)) >= PAD
        rel = jnp.clip(qpos[:, :, None] - kpos[:, None, :], -MAX_REL, MAX_REL) + MAX_REL
        bias = jnp.take(rel_bias, rel, axis=1).transpose(1, 0, 2, 3)
        s = jnp.einsum('bhqd,bhkd->bhqk', qc, kb).astype(jnp.float32) * scale
        s = s + bias.astype(jnp.float32)
        s = jnp.where(valid[None, None, None, :], s, NEG_INF)
        p = jax.nn.softmax(s, axis=-1)
        return jnp.einsum('bhqk,bhkd->bhqd', p.astype(vb.dtype), vb)

    out = lax.map(one_chunk, (qt, jnp.arange(n_chunks)))
    return out.transpose(1, 0, 3, 2, 4).reshape(B, S, H * dh)


def diff_attention(q, k, v, lam):
    B, S, H, _, dh = q.shape
    n_blocks = S // Q_BLOCK
    scale = 1.0 / math.sqrt(dh)
    qb = q.transpose(0, 2, 3, 1, 4).reshape(B, H, 2, n_blocks, Q_BLOCK, dh).transpose(3, 0, 1, 2, 4, 5)
    kchunk = jnp.arange(S) // CHUNK

    def one_block(args):
        qblk, bi = args
        s = jnp.einsum('bhmqd,bhmkd->bhmqk', qblk, k).astype(jnp.float32) * scale
        qchunk = (bi * Q_BLOCK + jnp.arange(Q_BLOCK)) // CHUNK
        mask = kchunk[None, :] <= qchunk[:, None]
        s = jnp.where(mask[None, None, None], s, NEG_INF)
        p = jax.nn.softmax(s, axis=-1)
        a = p[:, :, 0] - lam * p[:, :, 1]
        return jnp.einsum('bhqk,bhkd->bhqd', a.astype(v.dtype), v)

    out = lax.map(one_block, (qb, jnp.arange(n_blocks)))
    return out.transpose(1, 0, 3, 2, 4).reshape(B, S, H, 2 * dh)


def setup_inputs(seed: int = 0) -> dict:
    key = jax.random.key(seed)
    ks = jax.random.split(key, 20)
    f32 = jnp.float32
    nrm = lambda k, shp, s: jax.random.normal(k, shp, f32) * s
    x = jax.random.normal(ks[0], (BATCH, SEQ, D_MODEL), f32)
    offset = jax.random.randint(ks[1], (BATCH, 1), 0, 64, dtype=jnp.int32) * CHUNK
    positions = (offset + jnp.arange(SEQ, dtype=jnp.int32)[None, :]).astype(jnp.int32)
    return {
        "x": x,
        "positions": positions,
        "a_norm_pre": 1.0 + nrm(ks[2], (N_LAYERS_A, D_MODEL), 0.01),
        "a_w_in": nrm(ks[3], (N_LAYERS_A, D_MODEL, 4 * A_WIDTH), D_MODEL ** -0.5),
        "a_rel_bias": nrm(ks[4], (N_LAYERS_A, A_HEADS, 2 * MAX_REL + 1), 0.5),
        "a_w_out": nrm(ks[5], (N_LAYERS_A, A_WIDTH, D_MODEL), A_WIDTH ** -0.5),
        "a_norm_post": 1.0 + nrm(ks[6], (N_LAYERS_A, D_MODEL), 0.01),
        "kv_norm": 1.0 + nrm(ks[7], (D_MODEL,), 0.01),
        "kv_w": nrm(ks[8], (D_MODEL, 2 * B_WIDTH), D_MODEL ** -0.5),
        "b_norm_pre": 1.0 + nrm(ks[9], (N_LAYERS_B, D_MODEL), 0.01),
        "b_w_in": nrm(ks[10], (N_LAYERS_B, D_MODEL, 2 * B_WIDTH), D_MODEL ** -0.5),
        "b_lambda_q1": nrm(ks[11], (N_LAYERS_B, HEAD_DIM), 0.1),
        "b_lambda_k1": nrm(ks[12], (N_LAYERS_B, HEAD_DIM), 0.1),
        "b_lambda_q2": nrm(ks[13], (N_LAYERS_B, HEAD_DIM), 0.1),
        "b_lambda_k2": nrm(ks[14], (N_LAYERS_B, HEAD_DIM), 0.1),
        "b_subln": 1.0 + nrm(ks[15], (N_LAYERS_B, 2 * HEAD_DIM), 0.01),
        "b_w_out": nrm(ks[16], (N_LAYERS_B, B_WIDTH, D_MODEL), B_WIDTH ** -0.5),
        "b_norm_post": 1.0 + nrm(ks[17], (N_LAYERS_B, D_MODEL), 0.01),
    }


def reference(x, positions, a_norm_pre, a_w_in, a_rel_bias, a_w_out, a_norm_post,
              kv_norm, kv_w, b_norm_pre, b_w_in, b_lambda_q1, b_lambda_k1,
              b_lambda_q2, b_lambda_k2, b_subln, b_w_out, b_norm_post):
    B, S, _ = x.shape
    h = x
    k_sh = None
    v_sh = None
    for layer in range(DEPTH):
        if layer < N_LAYERS_A:
            i = layer
            u = rmsnorm(h, a_norm_pre[i])
            proj = u @ a_w_in[i]
            q, k, v, g = jnp.split(proj, 4, axis=-1)
            q = q.reshape(B, S, A_HEADS, HEAD_DIM)
            k = k.reshape(B, S, A_HEADS, HEAD_DIM)
            v = v.reshape(B, S, A_HEADS, HEAD_DIM)
            o = chunk_band_attention(q, k, v, positions, a_rel_bias[i])
            y = (o * jax.nn.silu(g)) @ a_w_out[i]
            h = h + rmsnorm(y, a_norm_post[i])
            if layer == N_LAYERS_A - 1:
                kv = rmsnorm(h, kv_norm) @ kv_w
                ks_, vs_ = jnp.split(kv, 2, axis=-1)
                ks_ = partial_rope(ks_.reshape(B, S, B_HEADS, 2, HEAD_DIM), positions)
                k_sh = ks_.transpose(0, 2, 3, 1, 4)
                v_sh = vs_.reshape(B, S, B_HEADS, 2 * HEAD_DIM).transpose(0, 2, 1, 3)
        else:
            j = layer - N_LAYERS_A
            lam_init = 0.8 - 0.6 * math.exp(-0.3 * layer)
            lam = (jnp.exp(jnp.sum(b_lambda_q1[j].astype(jnp.float32) * b_lambda_k1[j].astype(jnp.float32)))
                   - jnp.exp(jnp.sum(b_lambda_q2[j].astype(jnp.float32) * b_lambda_k2[j].astype(jnp.float32)))
                   + lam_init)
            u = rmsnorm(h, b_norm_pre[j])
            proj = u @ b_w_in[j]
            q, g = jnp.split(proj, 2, axis=-1)
            q = partial_rope(q.reshape(B, S, B_HEADS, 2, HEAD_DIM), positions)
            o = diff_attention(q, k_sh, v_sh, lam)
            o = rmsnorm(o, b_subln[j]) * (1.0 - lam_init)
            y = (o.reshape(B, S, B_WIDTH) * jax.nn.silu(g)) @ b_w_out[j]
            h = h + rmsnorm(y, b_norm_post[j])
    return h
```

```python
import functools
import math

import jax
import jax.numpy as jnp
import numpy as np
from jax import lax
from jax.experimental import pallas as pl
from jax.experimental.pallas import tpu as pltpu

D_MODEL = 1024
CHUNK = 64
PAST_CHUNKS = 8
HEAD_DIM = 64
A_HEADS = 16
B_HEADS = 8
MAX_REL = 128
ROT_DIM = 16
ROPE_THETA = 500000.0
RMS_EPS = 1e-6
NEG_INF = -1e30
LAM_INIT_B = 0.8 - 0.6 * math.exp(-0.3 * 1)

LANES = 128
N_GROUPS = D_MODEL // LANES
TM = 512
TQ = 256
KB = 256
A_KBLOCKS = (PAST_CHUNKS * CHUNK + TQ) // KB
VMEM_LIMIT = 56 * 1024 * 1024

_NT = (((1,), (1,)), ((), ()))


def _rms_scale(x):
    return lax.rsqrt(jnp.mean(x * x, axis=-1, keepdims=True) + RMS_EPS)


def _sigmoid(x):
    return 1.0 / (1.0 + jnp.exp(-x))


def _store_groups(o_ref, base, r):
    for gi in range(N_GROUPS):
        o_ref[base + gi] = r[:, gi * LANES:(gi + 1) * LANES].astype(o_ref.dtype)


def _proj_a_kernel(x_ref, g_ref, w_ref, o_ref):
    x = x_ref[...]
    y = (x * _rms_scale(x) * g_ref[...]).astype(jnp.bfloat16)
    for c in range(4):
        r = jnp.dot(y, w_ref[:, c * D_MODEL:(c + 1) * D_MODEL], preferred_element_type=jnp.float32)
        if c == 0:
            r = r * (1.0 / math.sqrt(HEAD_DIM))
        _store_groups(o_ref, c * N_GROUPS, r)


def _proj_a(x, gain, w):
    B, S, _ = x.shape
    return pl.pallas_call(
        _proj_a_kernel,
        grid=(B, S // TM),
        in_specs=[
            pl.BlockSpec((None, TM, D_MODEL), lambda b, i: (b, i, 0)),
            pl.BlockSpec((1, D_MODEL), lambda b, i: (0, 0)),
            pl.BlockSpec((D_MODEL, 4 * D_MODEL), lambda b, i: (0, 0)),
        ],
        out_specs=pl.BlockSpec((None, 4 * N_GROUPS, TM, LANES), lambda b, i: (b, 0, i, 0)),
        out_shape=jax.ShapeDtypeStruct((B, 4 * N_GROUPS, S, LANES), jnp.bfloat16),
        compiler_params=pltpu.CompilerParams(
            dimension_semantics=("arbitrary", "arbitrary"), vmem_limit_bytes=VMEM_LIMIT),
        name="proj_a",
    )(x, gain, w)


def _attn_a_kernel(q_ref, k_ref, v_ref, g_ref, x_ref, brow_ref, wout_ref, gpost_ref, o_ref,
                   bias_scr, y_scr):
    b = pl.program_id(0)
    i = pl.program_id(1)

    @pl.when((b == 0) & (i == 0))
    def _build_bias():
        r_chunk = lax.broadcasted_iota(jnp.int32, (TQ, A_KBLOCKS * KB), 0) // CHUNK
        j_chunk = lax.broadcasted_iota(jnp.int32, (TQ, A_KBLOCKS * KB), 1) // CHUNK
        in_band = (j_chunk >= r_chunk) & (j_chunk <= r_chunk + PAST_CHUNKS)
        for h in range(A_HEADS):
            row = jnp.broadcast_to(brow_ref[h:h + 1, :], (TQ, brow_ref.shape[1]))
            toep = pltpu.roll(row, 0, 1, stride=1, stride_axis=0)
            bias_scr[h] = jnp.where(in_band, toep[:, TQ:], NEG_INF)

    lane = lax.broadcasted_iota(jnp.int32, (TQ, LANES), 1)
    low = lane < HEAD_DIM

    def pair_body(hp, carry):
        q2 = q_ref[hp]
        outs = []
        for half in range(2):
            qm = jnp.where(low if half == 0 else ~low, q2, jnp.zeros_like(q2))
            h = 2 * hp + half
            s_blocks = []
            for t in range(A_KBLOCKS):
                kb = i - (A_KBLOCKS - 1) + t
                start = pl.multiple_of(jnp.maximum(kb, 0) * KB, KB)
                s = lax.dot_general(qm, k_ref[hp, pl.ds(start, KB), :], _NT,
                                    preferred_element_type=jnp.float32)
                s = s + bias_scr[h, :, t * KB:(t + 1) * KB]
                if t < A_KBLOCKS - 1:
                    s = jnp.where(kb >= 0, s, NEG_INF)
                s_blocks.append(s)
            m = functools.reduce(jnp.maximum, [jnp.max(s, axis=1, keepdims=True) for s in s_blocks])
            l = jnp.zeros((TQ, 1), jnp.float32)
            o = jnp.zeros((TQ, LANES), jnp.float32)
            for t in range(A_KBLOCKS):
                kb = i - (A_KBLOCKS - 1) + t
                start = pl.multiple_of(jnp.maximum(kb, 0) * KB, KB)
                p = jnp.exp(s_blocks[t] - m)
                l = l + jnp.sum(p, axis=1, keepdims=True)
                o = o + jnp.dot(p.astype(jnp.bfloat16), v_ref[hp, pl.ds(start, KB), :],
                                preferred_element_type=jnp.float32)
            outs.append(o * (1.0 / l))
        o_pair = jnp.where(low, outs[0], outs[1])
        gate = g_ref[hp].astype(jnp.float32)
        y_scr[hp] = (o_pair * (gate * _sigmoid(gate))).astype(jnp.bfloat16)
        return carry

    lax.fori_loop(0, N_GROUPS, pair_body, 0)

    y = jnp.concatenate([y_scr[gi] for gi in range(N_GROUPS)], axis=1)
    yo = jnp.dot(y, wout_ref[...], preferred_element_type=jnp.float32)
    o_ref[...] = x_ref[...] + yo * _rms_scale(yo) * gpost_ref[...]


def _attn_a(qkvg, x, brow, wout, gpost):
    B, S, _ = x.shape
    return pl.pallas_call(
        _attn_a_kernel,
        grid=(B, S // TQ),
        in_specs=[
            pl.BlockSpec((None, N_GROUPS, TQ, LANES), lambda b, i: (b, 0, i, 0)),
            pl.BlockSpec((None, N_GROUPS, S, LANES), lambda b, i: (b, 1, 0, 0)),
            pl.BlockSpec((None, N_GROUPS, S, LANES), lambda b, i: (b, 2, 0, 0)),
            pl.BlockSpec((None, N_GROUPS, TQ, LANES), lambda b, i: (b, 3, i, 0)),
            pl.BlockSpec((None, TQ, D_MODEL), lambda b, i: (b, i, 0)),
            pl.BlockSpec(brow.shape, lambda b, i: (0, 0)),
            pl.BlockSpec((D_MODEL, D_MODEL), lambda b, i: (0, 0)),
            pl.BlockSpec((1, D_MODEL), lambda b, i: (0, 0)),
        ],
        out_specs=pl.BlockSpec((None, TQ, D_MODEL), lambda b, i: (b, i, 0)),
        out_shape=jax.ShapeDtypeStruct((B, S, D_MODEL), jnp.float32),
        scratch_shapes=[
            pltpu.VMEM((A_HEADS, TQ, A_KBLOCKS * KB), jnp.float32),
            pltpu.VMEM((N_GROUPS, TQ, LANES), jnp.bfloat16),
        ],
        compiler_params=pltpu.CompilerParams(
            dimension_semantics=("arbitrary", "arbitrary"), vmem_limit_bytes=VMEM_LIMIT),
        name="attn_a",
    )(qkvg, qkvg, qkvg, qkvg, x, brow, wout, gpost)


def _rope(t, cos, sin_lo, sin_hi):
    return t * cos + pltpu.roll(t, LANES - ROT_DIM // 2, 1) * sin_lo + pltpu.roll(t, ROT_DIM // 2, 1) * sin_hi


def _proj_b_kernel(x_ref, pos_ref, freq_ref, gkv_ref, gb_ref, wkv_ref, wb_ref, o_ref):
    x = x_ref[...]
    xn = x * _rms_scale(x)
    u_kv = (xn * gkv_ref[...]).astype(jnp.bfloat16)
    u_b = (xn * gb_ref[...]).astype(jnp.bfloat16)

    ang = pos_ref[...].astype(jnp.float32) * freq_ref[...]
    cos = jnp.cos(ang)
    sin = jnp.sin(ang)
    lane = lax.broadcasted_iota(jnp.int32, ang.shape, 1) % HEAD_DIM
    sin_lo = jnp.where(lane < ROT_DIM // 2, -sin, 0.0)
    sin_hi = jnp.where((lane >= ROT_DIM // 2) & (lane < ROT_DIM), sin, 0.0)

    def rope_store(base, r, scale):
        for gi in range(N_GROUPS):
            t = _rope(r[:, gi * LANES:(gi + 1) * LANES], cos, sin_lo, sin_hi)
            if scale != 1.0:
                t = t * scale
            o_ref[base + gi] = t.astype(o_ref.dtype)

    k = jnp.dot(u_kv, wkv_ref[:, :D_MODEL], preferred_element_type=jnp.float32)
    rope_store(0, k, 1.0)
    v = jnp.dot(u_kv, wkv_ref[:, D_MODEL:], preferred_element_type=jnp.float32)
    _store_groups(o_ref, N_GROUPS, v)
    q = jnp.dot(u_b, wb_ref[:, :D_MODEL], preferred_element_type=jnp.float32)
    rope_store(2 * N_GROUPS, q, 1.0 / math.sqrt(HEAD_DIM))
    g = jnp.dot(u_b, wb_ref[:, D_MODEL:], preferred_element_type=jnp.float32)
    _store_groups(o_ref, 3 * N_GROUPS, g)


def _proj_b(h, pos, freq, gkv, gb, wkv, wb):
    B, S, _ = h.shape
    return pl.pallas_call(
        _proj_b_kernel,
        grid=(B, S // TM),
        in_specs=[
            pl.BlockSpec((None, TM, D_MODEL), lambda b, i: (b, i, 0)),
            pl.BlockSpec((None, TM, 1), lambda b, i: (b, i, 0)),
            pl.BlockSpec((1, LANES), lambda b, i: (0, 0)),
            pl.BlockSpec((1, D_MODEL), lambda b, i: (0, 0)),
            pl.BlockSpec((1, D_MODEL), lambda b, i: (0, 0)),
            pl.BlockSpec((D_MODEL, 2 * D_MODEL), lambda b, i: (0, 0)),
            pl.BlockSpec((D_MODEL, 2 * D_MODEL), lambda b, i: (0, 0)),
        ],
        out_specs=pl.BlockSpec((None, 4 * N_GROUPS, TM, LANES), lambda b, i: (b, 0, i, 0)),
        out_shape=jax.ShapeDtypeStruct((B, 4 * N_GROUPS, S, LANES), jnp.bfloat16),
        compiler_params=pltpu.CompilerParams(
            dimension_semantics=("arbitrary", "arbitrary"), vmem_limit_bytes=VMEM_LIMIT),
        name="proj_b",
    )(h, pos, freq, gkv, gb, wkv, wb)


def _attn_b_kernel(k_ref, v_ref, q_ref, g_ref, x_ref, lam_ref, subln_ref, wout_ref, gpost_ref, o_ref,
                   s_scr, m_scr, l_scr, acc_scr, y_scr):
    i = pl.program_id(1)

    lp = lam_ref[...]
    lam = (jnp.exp(jnp.sum(lp[0:1] * lp[1:2], axis=1, keepdims=True))
           - jnp.exp(jnp.sum(lp[2:3] * lp[3:4], axis=1, keepdims=True)) + LAM_INIT_B)

    lane = lax.broadcasted_iota(jnp.int32, (TQ, LANES), 1)
    low = lane < HEAD_DIM
    q_chunk = lax.broadcasted_iota(jnp.int32, (2 * TQ, KB), 0) % TQ // CHUNK
    k_chunk = lax.broadcasted_iota(jnp.int32, (2 * TQ, KB), 1) // CHUNK
    diag_mask = k_chunk <= q_chunk

    def head_body(h, carry):
        q2 = q_ref[h]
        zero = jnp.zeros_like(q2)
        qs = jnp.concatenate([jnp.where(low, q2, zero), jnp.where(low, zero, q2)], axis=0)

        def scores(j):
            start = pl.multiple_of(j * KB, KB)
            return lax.dot_general(qs, k_ref[h, pl.ds(start, KB), :], _NT,
                                   preferred_element_type=jnp.float32)

        m_scr[...] = jnp.full(m_scr.shape, NEG_INF, jnp.float32)

        def pass1(j, c):
            s = scores(j)
            s_scr[j] = s
            m_scr[...] = jnp.maximum(m_scr[...], jnp.max(s, axis=1, keepdims=True))
            return c

        lax.fori_loop(0, i, pass1, 0)
        s = jnp.where(diag_mask, scores(i), NEG_INF)
        s_scr[i] = s
        m = jnp.maximum(m_scr[...], jnp.max(s, axis=1, keepdims=True))

        l_scr[...] = jnp.zeros(l_scr.shape, jnp.float32)

        def pass2(j, c):
            p = jnp.exp(s_scr[j] - m)
            s_scr[j] = p
            l_scr[...] = l_scr[...] + jnp.sum(p, axis=1, keepdims=True)
            return c

        lax.fori_loop(0, i + 1, pass2, 0)
        l = l_scr[...]
        c1 = 1.0 / l[:TQ]
        c2 = lam / l[TQ:]

        acc_scr[...] = jnp.zeros(acc_scr.shape, jnp.float32)

        def pass3(j, c):
            start = pl.multiple_of(j * KB, KB)
            a = (s_scr[j, :TQ, :] * c1 - s_scr[j, TQ:, :] * c2).astype(jnp.bfloat16)
            acc_scr[...] = acc_scr[...] + jnp.dot(a, v_ref[h, pl.ds(start, KB), :],
                                                  preferred_element_type=jnp.float32)
            return c

        lax.fori_loop(0, i + 1, pass3, 0)
        o = acc_scr[...]
        o = o * _rms_scale(o) * subln_ref[...] * (1.0 - LAM_INIT_B)
        gate = g_ref[h].astype(jnp.float32)
        y_scr[h] = (o * (gate * _sigmoid(gate))).astype(jnp.bfloat16)
        return carry

    lax.fori_loop(0, B_HEADS, head_body, 0)

    y = jnp.concatenate([y_scr[gi] for gi in range(N_GROUPS)], axis=1)
    yo = jnp.dot(y, wout_ref[...], preferred_element_type=jnp.float32)
    o_ref[...] = x_ref[...] + yo * _rms_scale(yo) * gpost_ref[...]


def _attn_b(kvqg, h, lam_rows, subln, wout, gpost):
    B, S, _ = h.shape
    return pl.pallas_call(
        _attn_b_kernel,
        grid=(B, S // TQ),
        in_specs=[
            pl.BlockSpec((None, N_GROUPS, S, LANES), lambda b, i: (b, 0, 0, 0)),
            pl.BlockSpec((None, N_GROUPS, S, LANES), lambda b, i: (b, 1, 0, 0)),
            pl.BlockSpec((None, N_GROUPS, TQ, LANES), lambda b, i: (b, 2, i, 0)),
            pl.BlockSpec((None, N_GROUPS, TQ, LANES), lambda b, i: (b, 3, i, 0)),
            pl.BlockSpec((None, TQ, D_MODEL), lambda b, i: (b, i, 0)),
            pl.BlockSpec(lam_rows.shape, lambda b, i: (0, 0)),
            pl.BlockSpec((1, LANES), lambda b, i: (0, 0)),
            pl.BlockSpec((D_MODEL, D_MODEL), lambda b, i: (0, 0)),
            pl.BlockSpec((1, D_MODEL), lambda b, i: (0, 0)),
        ],
        out_specs=pl.BlockSpec((None, TQ, D_MODEL), lambda b, i: (b, i, 0)),
        out_shape=jax.ShapeDtypeStruct((B, S, D_MODEL), jnp.float32),
        scratch_shapes=[
            pltpu.VMEM((S // KB, 2 * TQ, KB), jnp.float32),
            pltpu.VMEM((2 * TQ, 1), jnp.float32),
            pltpu.VMEM((2 * TQ, 1), jnp.float32),
            pltpu.VMEM((TQ, LANES), jnp.float32),
            pltpu.VMEM((N_GROUPS, TQ, LANES), jnp.bfloat16),
        ],
        compiler_params=pltpu.CompilerParams(
            dimension_semantics=("arbitrary", "arbitrary"), vmem_limit_bytes=VMEM_LIMIT),
        name="attn_b",
    )(kvqg, kvqg, kvqg, kvqg, h, lam_rows, subln, wout, gpost)


def kernel(x, positions, a_norm_pre, a_w_in, a_rel_bias, a_w_out, a_norm_post, kv_norm, kv_w,
           b_norm_pre, b_w_in, b_lambda_q1, b_lambda_k1, b_lambda_q2, b_lambda_k2, b_subln, b_w_out,
           b_norm_post):
    assert a_w_in.shape[0] == 1 and b_w_in.shape[0] == 1
    bf16 = jnp.bfloat16

    m = np.arange(4 * TQ)
    idx = np.clip(3 * TQ - m, -MAX_REL, MAX_REL) + MAX_REL
    brow = jnp.take(a_rel_bias[0], jnp.asarray(idx, jnp.int32), axis=1)

    half = ROT_DIM // 2
    inv_freq = jnp.power(jnp.float32(ROPE_THETA), -jnp.arange(half, dtype=jnp.float32) * 2.0 / ROT_DIM)
    head_freq = jnp.concatenate([inv_freq, inv_freq, jnp.zeros((HEAD_DIM - ROT_DIM,), jnp.float32)])
    freq = jnp.tile(head_freq, LANES // HEAD_DIM)[None, :]

    lam_rows = jnp.zeros((8, LANES), jnp.float32)
    lam_rows = lam_rows.at[0:4, :HEAD_DIM].set(
        jnp.stack([b_lambda_q1[0], b_lambda_k1[0], b_lambda_q2[0], b_lambda_k2[0]]).astype(jnp.float32))

    qkvg = _proj_a(x, a_norm_pre[0][None, :], a_w_in[0].astype(bf16))
    h1 = _attn_a(qkvg, x, brow, a_w_out[0].astype(bf16), a_norm_post[0][None, :])
    kvqg = _proj_b(h1, positions[..., None], freq, kv_norm[None, :], b_norm_pre[0][None, :],
                   kv_w.astype(bf16), b_w_in[0].astype(bf16))
    out = _attn_b(kvqg, h1, lam_rows, b_subln[0][None, :], b_w_out[0].astype(bf16),
                  b_norm_post[0][None, :])
    return out
```

```python
import functools
import math

import jax
import jax.numpy as jnp
import numpy as np
from jax import lax
from jax.experimental import pallas as pl
from jax.experimental.pallas import tpu as pltpu

D_MODEL = 1024
CHUNK = 64
PAST_CHUNKS = 8
HEAD_DIM = 64
A_HEADS = 16
B_HEADS = 8
MAX_REL = 128
ROT_DIM = 16
ROPE_THETA = 500000.0
RMS_EPS = 1e-6
NEG_INF = -1e30
LAM_INIT_B = 0.8 - 0.6 * math.exp(-0.3 * 1)

LANES = 128
N_GROUPS = D_MODEL // LANES
TM = 512
TQ = 256
A_BAND = PAST_CHUNKS * CHUNK + TQ
VMEM_LIMIT = 56 * 1024 * 1024

_NT = (((1,), (1,)), ((), ()))
_ARB2 = pltpu.CompilerParams(dimension_semantics=("arbitrary", "arbitrary"),
                             vmem_limit_bytes=VMEM_LIMIT)


def _rms_scale(x):
    return lax.rsqrt(jnp.mean(x * x, axis=-1, keepdims=True) + RMS_EPS)


def _sigmoid(x):
    return 1.0 / (1.0 + jnp.exp(-x))


def _store_groups(o_ref, base, r):
    for gi in range(N_GROUPS):
        o_ref[base + gi] = r[:, gi * LANES:(gi + 1) * LANES].astype(o_ref.dtype)


def _proj_a_kernel(x_ref, g_ref, w_ref, o_ref):
    x = x_ref[...]
    y = (x * _rms_scale(x) * g_ref[...]).astype(jnp.bfloat16)
    for c in range(4):
        r = jnp.dot(y, w_ref[:, c * D_MODEL:(c + 1) * D_MODEL], preferred_element_type=jnp.float32)
        if c == 0:
            r = r * (1.0 / math.sqrt(HEAD_DIM))
        _store_groups(o_ref, c * N_GROUPS, r)


def _proj_a(x, gain, w):
    B, S, _ = x.shape
    return pl.pallas_call(
        _proj_a_kernel,
        grid=(B, S // TM),
        in_specs=[
            pl.BlockSpec((None, TM, D_MODEL), lambda b, i: (b, i, 0)),
            pl.BlockSpec((1, D_MODEL), lambda b, i: (0, 0)),
            pl.BlockSpec((D_MODEL, 4 * D_MODEL), lambda b, i: (0, 0)),
        ],
        out_specs=pl.BlockSpec((None, 4 * N_GROUPS, TM, LANES), lambda b, i: (b, 0, i, 0)),
        out_shape=jax.ShapeDtypeStruct((B, 4 * N_GROUPS, S, LANES), jnp.bfloat16),
        compiler_params=_ARB2,
        name="proj_a",
    )(x, gain, w)


def _attn_a_kernel(q_ref, k_ref, v_ref, g_ref, brow_ref, y_ref, bias_scr):
    b = pl.program_id(0)
    hp = pl.program_id(1)
    S = q_ref.shape[0]

    @pl.when((b == 0) & (hp == 0))
    def _build_bias():
        r_chunk = lax.broadcasted_iota(jnp.int32, (TQ, A_BAND), 0) // CHUNK
        j_chunk = lax.broadcasted_iota(jnp.int32, (TQ, A_BAND), 1) // CHUNK
        in_band = (j_chunk >= r_chunk) & (j_chunk <= r_chunk + PAST_CHUNKS)
        for h in range(A_HEADS):
            row = jnp.broadcast_to(brow_ref[h:h + 1, :], (TQ, brow_ref.shape[1]))
            toep = pltpu.roll(row, 0, 1, stride=1, stride_axis=0)
            bias_scr[h] = jnp.where(in_band, toep[:, TQ:], NEG_INF)

    low = lax.broadcasted_iota(jnp.int32, (TQ, LANES), 1) < HEAD_DIM

    for qi in range(S // TQ):
        rows = slice(qi * TQ, (qi + 1) * TQ)
        keys = slice(max(0, (qi + 1) * TQ - A_BAND), (qi + 1) * TQ)
        nk = keys.stop - keys.start
        q2 = q_ref[rows, :]
        outs = []
        for half in range(2):
            qm = jnp.where(low if half == 0 else ~low, q2, jnp.zeros_like(q2))
            s = lax.dot_general(qm, k_ref[keys, :], _NT, preferred_element_type=jnp.float32)
            s = s + bias_scr[2 * hp + half, :, A_BAND - nk:]
            m = jnp.max(s, axis=1, keepdims=True)
            p = jnp.exp(s - m)
            l = jnp.sum(p, axis=1, keepdims=True)
            o = jnp.dot(p.astype(jnp.bfloat16), v_ref[keys, :], preferred_element_type=jnp.float32)
            outs.append(o * (1.0 / l))
        o_pair = jnp.where(low, outs[0], outs[1])
        gate = g_ref[rows, :].astype(jnp.float32)
        y_ref[rows, :] = (o_pair * (gate * _sigmoid(gate))).astype(y_ref.dtype)


def _attn_a(qkvg, brow):
    B, _, S, _ = qkvg.shape

    def group(c):
        return pl.BlockSpec((None, None, S, LANES), lambda b, h: (b, c * N_GROUPS + h, 0, 0))

    return pl.pallas_call(
        _attn_a_kernel,
        grid=(B, N_GROUPS),
        in_specs=[group(0), group(1), group(2), group(3),
                  pl.BlockSpec(brow.shape, lambda b, h: (0, 0))],
        out_specs=pl.BlockSpec((None, None, S, LANES), lambda b, h: (b, h, 0, 0)),
        out_shape=jax.ShapeDtypeStruct((B, N_GROUPS, S, LANES), jnp.bfloat16),
        scratch_shapes=[pltpu.VMEM((A_HEADS, TQ, A_BAND), jnp.float32)],
        compiler_params=_ARB2,
        name="attn_a",
    )(qkvg, qkvg, qkvg, qkvg, brow)


def _out_kernel(y_ref, x_ref, w_ref, g_ref, o_ref):
    y = jnp.concatenate([y_ref[gi] for gi in range(N_GROUPS)], axis=1)
    yo = jnp.dot(y, w_ref[...], preferred_element_type=jnp.float32)
    o_ref[...] = x_ref[...] + yo * _rms_scale(yo) * g_ref[...]


def _out_proj(y, x, w, gain, name):
    B, S, _ = x.shape
    return pl.pallas_call(
        _out_kernel,
        grid=(B, S // TM),
        in_specs=[
            pl.BlockSpec((None, N_GROUPS, TM, LANES), lambda b, i: (b, 0, i, 0)),
            pl.BlockSpec((None, TM, D_MODEL), lambda b, i: (b, i, 0)),
            pl.BlockSpec((D_MODEL, D_MODEL), lambda b, i: (0, 0)),
            pl.BlockSpec((1, D_MODEL), lambda b, i: (0, 0)),
        ],
        out_specs=pl.BlockSpec((None, TM, D_MODEL), lambda b, i: (b, i, 0)),
        out_shape=jax.ShapeDtypeStruct((B, S, D_MODEL), jnp.float32),
        compiler_params=_ARB2,
        name=name,
    )(y, x, w, gain)


def _rope(t, cos, sin_lo, sin_hi):
    return t * cos + pltpu.roll(t, LANES - ROT_DIM // 2, 1) * sin_lo + pltpu.roll(t, ROT_DIM // 2, 1) * sin_hi


def _proj_b_kernel(x_ref, pos_ref, freq_ref, gkv_ref, gb_ref, wkv_ref, wb_ref, o_ref):
    x = x_ref[...]
    xn = x * _rms_scale(x)
    u_kv = (xn * gkv_ref[...]).astype(jnp.bfloat16)
    u_b = (xn * gb_ref[...]).astype(jnp.bfloat16)

    ang = pos_ref[...].astype(jnp.float32) * freq_ref[...]
    cos = jnp.cos(ang)
    sin = jnp.sin(ang)
    lane = lax.broadcasted_iota(jnp.int32, ang.shape, 1) % HEAD_DIM
    sin_lo = jnp.where(lane < ROT_DIM // 2, -sin, 0.0)
    sin_hi = jnp.where((lane >= ROT_DIM // 2) & (lane < ROT_DIM), sin, 0.0)

    def rope_store(base, r, scale):
        for gi in range(N_GROUPS):
            t = _rope(r[:, gi * LANES:(gi + 1) * LANES], cos, sin_lo, sin_hi)
            if scale != 1.0:
                t = t * scale
            o_ref[base + gi] = t.astype(o_ref.dtype)

    k = jnp.dot(u_kv, wkv_ref[:, :D_MODEL], preferred_element_type=jnp.float32)
    rope_store(0, k, 1.0)
    v = jnp.dot(u_kv, wkv_ref[:, D_MODEL:], preferred_element_type=jnp.float32)
    _store_groups(o_ref, N_GROUPS, v)
    q = jnp.dot(u_b, wb_ref[:, :D_MODEL], preferred_element_type=jnp.float32)
    rope_store(2 * N_GROUPS, q, 1.0 / math.sqrt(HEAD_DIM))
    g = jnp.dot(u_b, wb_ref[:, D_MODEL:], preferred_element_type=jnp.float32)
    _store_groups(o_ref, 3 * N_GROUPS, g)


def _proj_b(h, pos, freq, gkv, gb, wkv, wb):
    B, S, _ = h.shape
    return pl.pallas_call(
        _proj_b_kernel,
        grid=(B, S // TM),
        in_specs=[
            pl.BlockSpec((None, TM, D_MODEL), lambda b, i: (b, i, 0)),
            pl.BlockSpec((None, TM, 1), lambda b, i: (b, i, 0)),
            pl.BlockSpec((1, LANES), lambda b, i: (0, 0)),
            pl.BlockSpec((1, D_MODEL), lambda b, i: (0, 0)),
            pl.BlockSpec((1, D_MODEL), lambda b, i: (0, 0)),
            pl.BlockSpec((D_MODEL, 2 * D_MODEL), lambda b, i: (0, 0)),
            pl.BlockSpec((D_MODEL, 2 * D_MODEL), lambda b, i: (0, 0)),
        ],
        out_specs=pl.BlockSpec((None, 4 * N_GROUPS, TM, LANES), lambda b, i: (b, 0, i, 0)),
        out_shape=jax.ShapeDtypeStruct((B, 4 * N_GROUPS, S, LANES), jnp.bfloat16),
        compiler_params=_ARB2,
        name="proj_b",
    )(h, pos, freq, gkv, gb, wkv, wb)


def _attn_b_kernel(k_ref, v_ref, q_ref, g_ref, lam_ref, subln_ref, y_ref):
    S = q_ref.shape[0]
    lp = lam_ref[...]
    lam = (jnp.exp(jnp.sum(lp[0:1] * lp[1:2], axis=1, keepdims=True))
           - jnp.exp(jnp.sum(lp[2:3] * lp[3:4], axis=1, keepdims=True)) + LAM_INIT_B)

    low = lax.broadcasted_iota(jnp.int32, (TQ, LANES), 1) < HEAD_DIM
    q_chunk = lax.broadcasted_iota(jnp.int32, (2 * TQ, TQ), 0) % TQ // CHUNK
    k_chunk = lax.broadcasted_iota(jnp.int32, (2 * TQ, TQ), 1) // CHUNK
    diag_mask = k_chunk <= q_chunk

    for qi in range(S // TQ):
        rows = slice(qi * TQ, (qi + 1) * TQ)
        past = slice(0, qi * TQ)
        q2 = q_ref[rows, :]
        zero = jnp.zeros_like(q2)
        qs = jnp.concatenate([jnp.where(low, q2, zero), jnp.where(low, zero, q2)], axis=0)

        s_d = lax.dot_general(qs, k_ref[rows, :], _NT, preferred_element_type=jnp.float32)
        s_d = jnp.where(diag_mask, s_d, NEG_INF)
        m = jnp.max(s_d, axis=1, keepdims=True)
        if qi > 0:
            s_p = lax.dot_general(qs, k_ref[past, :], _NT, preferred_element_type=jnp.float32)
            m = jnp.maximum(m, jnp.max(s_p, axis=1, keepdims=True))
        p_d = jnp.exp(s_d - m)
        l = jnp.sum(p_d, axis=1, keepdims=True)
        if qi > 0:
            p_p = jnp.exp(s_p - m)
            l = l + jnp.sum(p_p, axis=1, keepdims=True)
        c1 = 1.0 / l[:TQ]
        c2 = lam / l[TQ:]
        a_d = (p_d[:TQ] * c1 - p_d[TQ:] * c2).astype(jnp.bfloat16)
        o = jnp.dot(a_d, v_ref[rows, :], preferred_element_type=jnp.float32)
        if qi > 0:
            a_p = (p_p[:TQ] * c1 - p_p[TQ:] * c2).astype(jnp.bfloat16)
            o = o + jnp.dot(a_p, v_ref[past, :], preferred_element_type=jnp.float32)

        o = o * _rms_scale(o) * subln_ref[...] * (1.0 - LAM_INIT_B)
        gate = g_ref[rows, :].astype(jnp.float32)
        y_ref[rows, :] = (o * (gate * _sigmoid(gate))).astype(y_ref.dtype)


def _attn_b(kvqg, lam_rows, subln):
    B, _, S, _ = kvqg.shape

    def group(c):
        return pl.BlockSpec((None, None, S, LANES), lambda b, h: (b, c * N_GROUPS + h, 0, 0))

    return pl.pallas_call(
        _attn_b_kernel,
        grid=(B, B_HEADS),
        in_specs=[group(0), group(1), group(2), group(3),
                  pl.BlockSpec(lam_rows.shape, lambda b, h: (0, 0)),
                  pl.BlockSpec((1, LANES), lambda b, h: (0, 0))],
        out_specs=pl.BlockSpec((None, None, S, LANES), lambda b, h: (b, h, 0, 0)),
        out_shape=jax.ShapeDtypeStruct((B, B_HEADS, S, LANES), jnp.bfloat16),
        compiler_params=_ARB2,
        name="attn_b",
    )(kvqg, kvqg, kvqg, kvqg, lam_rows, subln)


def kernel(x, positions, a_norm_pre, a_w_in, a_rel_bias, a_w_out, a_norm_post, kv_norm, kv_w,
           b_norm_pre, b_w_in, b_lambda_q1, b_lambda_k1, b_lambda_q2, b_lambda_k2, b_subln, b_w_out,
           b_norm_post):
    assert a_w_in.shape[0] == 1 and b_w_in.shape[0] == 1
    bf16 = jnp.bfloat16

    m = np.arange(A_BAND + TQ)
    idx = np.clip(A_BAND - m, -MAX_REL, MAX_REL) + MAX_REL
    brow = jnp.take(a_rel_bias[0], jnp.asarray(idx, jnp.int32), axis=1)

    half = ROT_DIM // 2
    inv_freq = jnp.power(jnp.float32(ROPE_THETA), -jnp.arange(half, dtype=jnp.float32) * 2.0 / ROT_DIM)
    head_freq = jnp.concatenate([inv_freq, inv_freq, jnp.zeros((HEAD_DIM - ROT_DIM,), jnp.float32)])
    freq = jnp.tile(head_freq, LANES // HEAD_DIM)[None, :]

    lam_rows = jnp.zeros((8, LANES), jnp.float32)
    lam_rows = lam_rows.at[0:4, :HEAD_DIM].set(
        jnp.stack([b_lambda_q1[0], b_lambda_k1[0], b_lambda_q2[0], b_lambda_k2[0]]).astype(jnp.float32))

    qkvg = _proj_a(x, a_norm_pre[0][None, :], a_w_in[0].astype(bf16))
    y_a = _attn_a(qkvg, brow)
    h1 = _out_proj(y_a, x, a_w_out[0].astype(bf16), a_norm_post[0][None, :], "out_a")
    kvqg = _proj_b(h1, positions[..., None], freq, kv_norm[None, :], b_norm_pre[0][None, :],
                   kv_w.astype(bf16), b_w_in[0].astype(bf16))
    y_b = _attn_b(kvqg, lam_rows, b_subln[0][None, :])
    return _out_proj(y_b, h1, b_w_out[0].astype(bf16), b_norm_post[0][None, :], "out_b")
```

```python
import functools
import math

import jax
import jax.numpy as jnp
import numpy as np
from jax import lax
from jax.experimental import pallas as pl
from jax.experimental.pallas import tpu as pltpu

D_MODEL = 1024
CHUNK = 64
PAST_CHUNKS = 8
HEAD_DIM = 64
A_HEADS = 16
B_HEADS = 8
MAX_REL = 128
ROT_DIM = 16
ROPE_THETA = 500000.0
RMS_EPS = 1e-6
NEG_INF = -1e30
LAM_INIT_B = 0.8 - 0.6 * math.exp(-0.3 * 1)

LANES = 128
N_GROUPS = D_MODEL // LANES
TM = 512
TQ = 256
A_BAND = PAST_CHUNKS * CHUNK + TQ
VMEM_LIMIT = 56 * 1024 * 1024

_NT = (((1,), (1,)), ((), ()))
_ARB2 = pltpu.CompilerParams(dimension_semantics=("arbitrary", "arbitrary"),
                             vmem_limit_bytes=VMEM_LIMIT)


def _rms_scale(x):
    return lax.rsqrt(jnp.mean(x * x, axis=-1, keepdims=True) + RMS_EPS)


def _sigmoid(x):
    return 1.0 / (1.0 + jnp.exp(-x))


def _store_groups(o_ref, base, r):
    for gi in range(N_GROUPS):
        o_ref[base + gi] = r[:, gi * LANES:(gi + 1) * LANES].astype(o_ref.dtype)


def _proj_a_kernel(x_ref, g_ref, w_ref, o_ref):
    x = x_ref[...]
    y = (x * _rms_scale(x) * g_ref[...]).astype(jnp.bfloat16)
    for c in range(4):
        r = jnp.dot(y, w_ref[:, c * D_MODEL:(c + 1) * D_MODEL], preferred_element_type=jnp.float32)
        if c == 0:
            r = r * (1.0 / math.sqrt(HEAD_DIM))
        _store_groups(o_ref, c * N_GROUPS, r)


def _proj_a(x, gain, w):
    B, S, _ = x.shape
    return pl.pallas_call(
        _proj_a_kernel,
        grid=(B, S // TM),
        in_specs=[
            pl.BlockSpec((None, TM, D_MODEL), lambda b, i: (b, i, 0)),
            pl.BlockSpec((1, D_MODEL), lambda b, i: (0, 0)),
            pl.BlockSpec((D_MODEL, 4 * D_MODEL), lambda b, i: (0, 0)),
        ],
        out_specs=pl.BlockSpec((None, 4 * N_GROUPS, TM, LANES), lambda b, i: (b, 0, i, 0)),
        out_shape=jax.ShapeDtypeStruct((B, 4 * N_GROUPS, S, LANES), jnp.bfloat16),
        compiler_params=_ARB2,
        name="proj_a",
    )(x, gain, w)


def _attn_a_kernel(q_ref, k_ref, v_ref, g_ref, brow_ref, y_ref, bias_scr, vx_scr, s_scr, p_scr):
    b = pl.program_id(0)
    hp = pl.program_id(1)
    S = q_ref.shape[0]

    @pl.when((b == 0) & (hp == 0))
    def _build_bias():
        r_chunk = lax.broadcasted_iota(jnp.int32, (TQ, A_BAND), 0) // CHUNK
        j_chunk = lax.broadcasted_iota(jnp.int32, (TQ, A_BAND), 1) // CHUNK
        in_band = (j_chunk >= r_chunk) & (j_chunk <= r_chunk + PAST_CHUNKS)
        for h in range(A_HEADS):
            row = jnp.broadcast_to(brow_ref[h:h + 1, :], (TQ, brow_ref.shape[1]))
            toep = pltpu.roll(row, 0, 1, stride=1, stride_axis=0)
            bias_scr[h] = jnp.where(in_band, toep[:, TQ:], NEG_INF)

    vx_scr[:, :LANES] = v_ref[...]
    vx_scr[:, LANES:] = jnp.ones((S, LANES), vx_scr.dtype)

    low = lax.broadcasted_iota(jnp.int32, (TQ, LANES), 1) < HEAD_DIM

    def keys_of(qi):
        return slice(max(0, (qi + 1) * TQ - A_BAND), (qi + 1) * TQ)

    def scores(n):
        qi, half = divmod(n, 2)
        keys = keys_of(qi)
        nk = keys.stop - keys.start
        q2 = q_ref[qi * TQ:(qi + 1) * TQ, :]
        qm = jnp.where(low if half == 0 else ~low, q2, jnp.zeros_like(q2))
        s = lax.dot_general(qm, k_ref[keys, :], _NT, preferred_element_type=jnp.float32)
        s_scr[n % 2, :, :nk] = s + bias_scr[2 * hp + half, :, A_BAND - nk:]

    def softmax(n):
        keys = keys_of(n // 2)
        nk = keys.stop - keys.start
        s = s_scr[n % 2, :, :nk]
        m = jnp.max(s, axis=1, keepdims=True)
        p_scr[n % 2, :, :nk] = jnp.exp(s - m).astype(p_scr.dtype)

    def values(n):
        keys = keys_of(n // 2)
        nk = keys.stop - keys.start
        ol = jnp.dot(p_scr[n % 2, :, :nk], vx_scr[keys, :], preferred_element_type=jnp.float32)
        return ol[:, :LANES] / ol[:, LANES:]

    n_stages = 2 * (S // TQ)
    order = list(range(n_stages - 1, -1, -1))
    scores(order[0])
    outs = {}
    for i, n in enumerate(order):
        if i + 1 < n_stages:
            scores(order[i + 1])
        softmax(n)
        outs[n] = values(n)
        if n % 2 == 0:
            rows = slice((n // 2) * TQ, (n // 2 + 1) * TQ)
            o_pair = jnp.where(low, outs.pop(n), outs.pop(n + 1))
            gate = g_ref[rows, :].astype(jnp.float32)
            y_ref[rows, :] = (o_pair * (gate * _sigmoid(gate))).astype(y_ref.dtype)


def _attn_a(qkvg, brow):
    B, _, S, _ = qkvg.shape

    def group(c):
        return pl.BlockSpec((None, None, S, LANES), lambda b, h: (b, c * N_GROUPS + h, 0, 0))

    return pl.pallas_call(
        _attn_a_kernel,
        grid=(B, N_GROUPS),
        in_specs=[group(0), group(1), group(2), group(3),
                  pl.BlockSpec(brow.shape, lambda b, h: (0, 0))],
        out_specs=pl.BlockSpec((None, None, S, LANES), lambda b, h: (b, h, 0, 0)),
        out_shape=jax.ShapeDtypeStruct((B, N_GROUPS, S, LANES), jnp.bfloat16),
        scratch_shapes=[
            pltpu.VMEM((A_HEADS, TQ, A_BAND), jnp.float32),
            pltpu.VMEM((S, 2 * LANES), jnp.bfloat16),
            pltpu.VMEM((2, TQ, A_BAND), jnp.float32),
            pltpu.VMEM((2, TQ, A_BAND), jnp.bfloat16),
        ],
        compiler_params=_ARB2,
        name="attn_a",
    )(qkvg, qkvg, qkvg, qkvg, brow)


def _out_kernel(y_ref, x_ref, w_ref, g_ref, o_ref):
    y = jnp.concatenate([y_ref[gi] for gi in range(N_GROUPS)], axis=1)
    yo = jnp.dot(y, w_ref[...], preferred_element_type=jnp.float32)
    o_ref[...] = x_ref[...] + yo * _rms_scale(yo) * g_ref[...]


def _out_proj(y, x, w, gain, name):
    B, S, _ = x.shape
    return pl.pallas_call(
        _out_kernel,
        grid=(B, S // TM),
        in_specs=[
            pl.BlockSpec((None, N_GROUPS, TM, LANES), lambda b, i: (b, 0, i, 0)),
            pl.BlockSpec((None, TM, D_MODEL), lambda b, i: (b, i, 0)),
            pl.BlockSpec((D_MODEL, D_MODEL), lambda b, i: (0, 0)),
            pl.BlockSpec((1, D_MODEL), lambda b, i: (0, 0)),
        ],
        out_specs=pl.BlockSpec((None, TM, D_MODEL), lambda b, i: (b, i, 0)),
        out_shape=jax.ShapeDtypeStruct((B, S, D_MODEL), jnp.float32),
        compiler_params=_ARB2,
        name=name,
    )(y, x, w, gain)


def _rope(t, cos, sin_lo, sin_hi):
    return t * cos + pltpu.roll(t, LANES - ROT_DIM // 2, 1) * sin_lo + pltpu.roll(t, ROT_DIM // 2, 1) * sin_hi


def _proj_b_kernel(x_ref, pos_ref, freq_ref, gkv_ref, gb_ref, wkv_ref, wb_ref, o_ref):
    x = x_ref[...]
    xn = x * _rms_scale(x)
    u_kv = (xn * gkv_ref[...]).astype(jnp.bfloat16)
    u_b = (xn * gb_ref[...]).astype(jnp.bfloat16)

    ang = pos_ref[...].astype(jnp.float32) * freq_ref[...]
    cos = jnp.cos(ang)
    sin = jnp.sin(ang)
    lane = lax.broadcasted_iota(jnp.int32, ang.shape, 1) % HEAD_DIM
    sin_lo = jnp.where(lane < ROT_DIM // 2, -sin, 0.0)
    sin_hi = jnp.where((lane >= ROT_DIM // 2) & (lane < ROT_DIM), sin, 0.0)

    def rope_store(base, r, scale):
        for gi in range(N_GROUPS):
            t = _rope(r[:, gi * LANES:(gi + 1) * LANES], cos, sin_lo, sin_hi)
            if scale != 1.0:
                t = t * scale
            o_ref[base + gi] = t.astype(o_ref.dtype)

    k = jnp.dot(u_kv, wkv_ref[:, :D_MODEL], preferred_element_type=jnp.float32)
    rope_store(0, k, 1.0)
    v = jnp.dot(u_kv, wkv_ref[:, D_MODEL:], preferred_element_type=jnp.float32)
    _store_groups(o_ref, N_GROUPS, v)
    q = jnp.dot(u_b, wb_ref[:, :D_MODEL], preferred_element_type=jnp.float32)
    rope_store(2 * N_GROUPS, q, 1.0 / math.sqrt(HEAD_DIM))
    g = jnp.dot(u_b, wb_ref[:, D_MODEL:], preferred_element_type=jnp.float32)
    _store_groups(o_ref, 3 * N_GROUPS, g)


def _proj_b(h, pos, freq, gkv, gb, wkv, wb):
    B, S, _ = h.shape
    return pl.pallas_call(
        _proj_b_kernel,
        grid=(B, S // TM),
        in_specs=[
            pl.BlockSpec((None, TM, D_MODEL), lambda b, i: (b, i, 0)),
            pl.BlockSpec((None, TM, 1), lambda b, i: (b, i, 0)),
            pl.BlockSpec((1, LANES), lambda b, i: (0, 0)),
            pl.BlockSpec((1, D_MODEL), lambda b, i: (0, 0)),
            pl.BlockSpec((1, D_MODEL), lambda b, i: (0, 0)),
            pl.BlockSpec((D_MODEL, 2 * D_MODEL), lambda b, i: (0, 0)),
            pl.BlockSpec((D_MODEL, 2 * D_MODEL), lambda b, i: (0, 0)),
        ],
        out_specs=pl.BlockSpec((None, 4 * N_GROUPS, TM, LANES), lambda b, i: (b, 0, i, 0)),
        out_shape=jax.ShapeDtypeStruct((B, 4 * N_GROUPS, S, LANES), jnp.bfloat16),
        compiler_params=_ARB2,
        name="proj_b",
    )(h, pos, freq, gkv, gb, wkv, wb)


def _attn_b_kernel(k_ref, v_ref, q_ref, g_ref, lam_ref, subln_ref, y_ref, vx_scr, s_scr, p_scr):
    S = q_ref.shape[0]
    n_q = S // TQ
    lp = lam_ref[...]
    lam = (jnp.exp(jnp.sum(lp[0:1] * lp[1:2], axis=1, keepdims=True))
           - jnp.exp(jnp.sum(lp[2:3] * lp[3:4], axis=1, keepdims=True)) + LAM_INIT_B)

    vx_scr[:, :LANES] = v_ref[...]
    vx_scr[:, LANES:] = jnp.ones((S, LANES), vx_scr.dtype)

    low = lax.broadcasted_iota(jnp.int32, (TQ, LANES), 1) < HEAD_DIM
    q_chunk = lax.broadcasted_iota(jnp.int32, (2 * TQ, TQ), 0) % TQ // CHUNK
    k_chunk = lax.broadcasted_iota(jnp.int32, (2 * TQ, TQ), 1) // CHUNK
    diag_mask = k_chunk <= q_chunk

    def scores(qi):
        slot = qi % 2
        q2 = q_ref[qi * TQ:(qi + 1) * TQ, :]
        zero = jnp.zeros_like(q2)
        qs = jnp.concatenate([jnp.where(low, q2, zero), jnp.where(low, zero, q2)], axis=0)
        if qi > 0:
            s_scr[slot, :, :qi * TQ] = lax.dot_general(qs, k_ref[:qi * TQ, :], _NT,
                                                       preferred_element_type=jnp.float32)
        s_d = lax.dot_general(qs, k_ref[qi * TQ:(qi + 1) * TQ, :], _NT, preferred_element_type=jnp.float32)
        s_scr[slot, :, qi * TQ:(qi + 1) * TQ] = jnp.where(diag_mask, s_d, NEG_INF)

    def softmax(qi):
        slot = qi % 2
        nk = (qi + 1) * TQ
        s = s_scr[slot, :, :nk]
        m = jnp.max(s, axis=1, keepdims=True)
        p_scr[slot, :, :nk] = jnp.exp(s - m).astype(p_scr.dtype)

    def values(qi):
        slot = qi % 2
        nk = (qi + 1) * TQ
        rows = slice(qi * TQ, (qi + 1) * TQ)
        ol = jnp.dot(p_scr[slot, :, :nk], vx_scr[:nk, :], preferred_element_type=jnp.float32)
        o = ol[:TQ, :LANES] / ol[:TQ, LANES:] - lam * (ol[TQ:, :LANES] / ol[TQ:, LANES:])
        o = o * _rms_scale(o) * subln_ref[...] * (1.0 - LAM_INIT_B)
        gate = g_ref[rows, :].astype(jnp.float32)
        y_ref[rows, :] = (o * (gate * _sigmoid(gate))).astype(y_ref.dtype)

    order = list(range(n_q - 1, -1, -1))
    scores(order[0])
    for n, qi in enumerate(order):
        if n + 1 < n_q:
            scores(order[n + 1])
        softmax(qi)
        values(qi)


def _attn_b(kvqg, lam_rows, subln):
    B, _, S, _ = kvqg.shape

    def group(c):
        return pl.BlockSpec((None, None, S, LANES), lambda b, h: (b, c * N_GROUPS + h, 0, 0))

    return pl.pallas_call(
        _attn_b_kernel,
        grid=(B, B_HEADS),
        in_specs=[group(0), group(1), group(2), group(3),
                  pl.BlockSpec(lam_rows.shape, lambda b, h: (0, 0)),
                  pl.BlockSpec((1, LANES), lambda b, h: (0, 0))],
        out_specs=pl.BlockSpec((None, None, S, LANES), lambda b, h: (b, h, 0, 0)),
        out_shape=jax.ShapeDtypeStruct((B, B_HEADS, S, LANES), jnp.bfloat16),
        scratch_shapes=[
            pltpu.VMEM((S, 2 * LANES), jnp.bfloat16),
            pltpu.VMEM((2, 2 * TQ, S), jnp.float32),
            pltpu.VMEM((2, 2 * TQ, S), jnp.bfloat16),
        ],
        compiler_params=_ARB2,
        name="attn_b",
    )(kvqg, kvqg, kvqg, kvqg, lam_rows, subln)


def kernel(x, positions, a_norm_pre, a_w_in, a_rel_bias, a_w_out, a_norm_post, kv_norm, kv_w,
           b_norm_pre, b_w_in, b_lambda_q1, b_lambda_k1, b_lambda_q2, b_lambda_k2, b_subln, b_w_out,
           b_norm_post):
    assert a_w_in.shape[0] == 1 and b_w_in.shape[0] == 1
    bf16 = jnp.bfloat16

    m = np.arange(A_BAND + TQ)
    idx = np.clip(A_BAND - m, -MAX_REL, MAX_REL) + MAX_REL
    brow = jnp.take(a_rel_bias[0], jnp.asarray(idx, jnp.int32), axis=1)

    half = ROT_DIM // 2
    inv_freq = jnp.power(jnp.float32(ROPE_THETA), -jnp.arange(half, dtype=jnp.float32) * 2.0 / ROT_DIM)
    head_freq = jnp.concatenate([inv_freq, inv_freq, jnp.zeros((HEAD_DIM - ROT_DIM,), jnp.float32)])
    freq = jnp.tile(head_freq, LANES // HEAD_DIM)[None, :]

    lam_rows = jnp.zeros((8, LANES), jnp.float32)
    lam_rows = lam_rows.at[0:4, :HEAD_DIM].set(
        jnp.stack([b_lambda_q1[0], b_lambda_k1[0], b_lambda_q2[0], b_lambda_k2[0]]).astype(jnp.float32))

    qkvg = _proj_a(x, a_norm_pre[0][None, :], a_w_in[0].astype(bf16))
    y_a = _attn_a(qkvg, brow)
    h1 = _out_proj(y_a, x, a_w_out[0].astype(bf16), a_norm_post[0][None, :], "out_a")
    kvqg = _proj_b(h1, positions[..., None], freq, kv_norm[None, :], b_norm_pre[0][None, :],
                   kv_w.astype(bf16), b_w_in[0].astype(bf16))
    y_b = _attn_b(kvqg, lam_rows, b_subln[0][None, :])
    return _out_proj(y_b, h1, b_w_out[0].astype(bf16), b_norm_post[0][None, :], "out_b")
```

```python
import math

import jax
import jax.numpy as jnp
import numpy as np
from jax import lax
from jax.experimental import pallas as pl
from jax.experimental.pallas import tpu as pltpu

D_MODEL = 1024
CHUNK = 64
PAST_CHUNKS = 8
HEAD_DIM = 64
A_HEADS = 16
B_HEADS = 8
MAX_REL = 128
ROT_DIM = 16
ROPE_THETA = 500000.0
RMS_EPS = 1e-6
NEG_INF = -1e30
LAM_INIT_B = 0.8 - 0.6 * math.exp(-0.3 * 1)
LOG2E = math.log2(math.e)
Q_SCALE = LOG2E / math.sqrt(HEAD_DIM)

LANES = 128
N_GROUPS = D_MODEL // LANES
TM = 512
TQ = 256
A_BAND = PAST_CHUNKS * CHUNK + TQ
ONES_ROWS = 16
VMEM_LIMIT = 56 * 1024 * 1024

_NT = (((1,), (1,)), ((), ()))
_ARB2 = pltpu.CompilerParams(dimension_semantics=("arbitrary", "arbitrary"),
                             vmem_limit_bytes=VMEM_LIMIT)


def _rms_scale(x):
    return lax.rsqrt(jnp.mean(x * x, axis=-1, keepdims=True) + RMS_EPS)


def _sigmoid(x):
    return 1.0 / (1.0 + jnp.exp(-x))


def _store_groups(o_ref, base, r):
    for gi in range(N_GROUPS):
        o_ref[base + gi] = r[:, gi * LANES:(gi + 1) * LANES].astype(o_ref.dtype)


def _store_transposed(vt_ref, w_t_ref, u):
    vt = lax.dot_general(w_t_ref[...], u, _NT, preferred_element_type=jnp.float32)
    for gi in range(N_GROUPS):
        vt_ref[gi] = vt[gi * LANES:(gi + 1) * LANES, :].astype(vt_ref.dtype)


def _group_spec(S, slab):
    return pl.BlockSpec((None, None, S, LANES), lambda b, h: (b, slab * N_GROUPS + h, 0, 0))


def _vt_spec(S):
    return pl.BlockSpec((None, None, LANES, S), lambda b, h: (b, h, 0, 0))


def _stack_pair(q2, low):
    zero = jnp.zeros_like(q2)
    return jnp.concatenate([jnp.where(low, q2, zero), jnp.where(low, zero, q2)], axis=0)


def _proj_a_kernel(x_ref, g_ref, w_ref, wvt_ref, o_ref, vt_ref):
    x = x_ref[...]
    y = (x * _rms_scale(x) * g_ref[...]).astype(jnp.bfloat16)
    for c in range(3):
        r = jnp.dot(y, w_ref[:, c * D_MODEL:(c + 1) * D_MODEL], preferred_element_type=jnp.float32)
        if c == 0:
            r = r * Q_SCALE
        _store_groups(o_ref, c * N_GROUPS, r)
    _store_transposed(vt_ref, wvt_ref, y)


def _proj_a(x, gain, w_qkg, w_vt):
    B, S, _ = x.shape
    return pl.pallas_call(
        _proj_a_kernel,
        grid=(B, S // TM),
        in_specs=[
            pl.BlockSpec((None, TM, D_MODEL), lambda b, i: (b, i, 0)),
            pl.BlockSpec((1, D_MODEL), lambda b, i: (0, 0)),
            pl.BlockSpec((D_MODEL, 3 * D_MODEL), lambda b, i: (0, 0)),
            pl.BlockSpec((D_MODEL, D_MODEL), lambda b, i: (0, 0)),
        ],
        out_specs=[
            pl.BlockSpec((None, 3 * N_GROUPS, TM, LANES), lambda b, i: (b, 0, i, 0)),
            pl.BlockSpec((None, N_GROUPS, LANES, TM), lambda b, i: (b, 0, 0, i)),
        ],
        out_shape=[
            jax.ShapeDtypeStruct((B, 3 * N_GROUPS, S, LANES), jnp.bfloat16),
            jax.ShapeDtypeStruct((B, N_GROUPS, LANES, S), jnp.bfloat16),
        ],
        compiler_params=_ARB2,
        name="proj_a",
    )(x, gain, w_qkg, w_vt)


def _attn_a_kernel(q_ref, k_ref, g_ref, vt_ref, brow_ref, y_ref, bias_scr, vx_scr, s_scr, p_scr):
    b = pl.program_id(0)
    hp = pl.program_id(1)
    S = q_ref.shape[0]
    n_q = S // TQ

    @pl.when((b == 0) & (hp == 0))
    def _build_bias():
        j_chunk = lax.broadcasted_iota(jnp.int32, (A_BAND, TQ), 0) // CHUNK
        r_chunk = lax.broadcasted_iota(jnp.int32, (A_BAND, TQ), 1) // CHUNK
        in_band = (j_chunk >= r_chunk) & (j_chunk <= r_chunk + PAST_CHUNKS)
        for h in range(A_HEADS):
            row = jnp.broadcast_to(brow_ref[h:h + 1, :], (A_BAND, brow_ref.shape[1]))
            toep = pltpu.roll(row, TQ, 1, stride=1, stride_axis=0)
            bias_scr[h] = jnp.where(in_band, toep[:, :TQ] * LOG2E, NEG_INF)

    vx_scr[:LANES, :] = vt_ref[...]
    vx_scr[LANES:, :] = jnp.ones((ONES_ROWS, S), vx_scr.dtype)

    low = lax.broadcasted_iota(jnp.int32, (TQ, LANES), 1) < HEAD_DIM

    def keys_of(qi):
        return slice(max(0, (qi + 1) * TQ - A_BAND), (qi + 1) * TQ)

    def scores(qi):
        keys = keys_of(qi)
        nk = keys.stop - keys.start
        qs = _stack_pair(q_ref[qi * TQ:(qi + 1) * TQ, :], low)
        s = lax.dot_general(k_ref[keys, :], qs, _NT, preferred_element_type=jnp.float32)
        s_scr[qi % 2, :nk, :TQ] = s[:, :TQ] + bias_scr[2 * hp, A_BAND - nk:, :]
        s_scr[qi % 2, :nk, TQ:] = s[:, TQ:] + bias_scr[2 * hp + 1, A_BAND - nk:, :]

    def softmax(qi):
        nk = keys_of(qi).stop - keys_of(qi).start
        s = s_scr[qi % 2, :nk, :]
        m = jnp.max(s, axis=0, keepdims=True)
        p_scr[qi % 2, :nk, :] = jnp.exp2(s - m).astype(p_scr.dtype)

    def values(qi):
        keys = keys_of(qi)
        nk = keys.stop - keys.start
        rows = slice(qi * TQ, (qi + 1) * TQ)
        ol = jnp.dot(vx_scr[:, keys], p_scr[qi % 2, :nk, :], preferred_element_type=jnp.float32)
        o_t = jnp.concatenate([ol[:HEAD_DIM, :TQ] / ol[LANES:LANES + 1, :TQ],
                               ol[HEAD_DIM:LANES, TQ:] / ol[LANES:LANES + 1, TQ:]], axis=0)
        gate = g_ref[rows, :].astype(jnp.float32)
        y_ref[rows, :] = (o_t.T * (gate * _sigmoid(gate))).astype(y_ref.dtype)

    order = list(range(n_q - 1, -1, -1))
    scores(order[0])
    for n, qi in enumerate(order):
        if n + 1 < n_q:
            scores(order[n + 1])
        softmax(qi)
        values(qi)


def _attn_a(qkg, vt, brow):
    B, _, S, _ = qkg.shape
    return pl.pallas_call(
        _attn_a_kernel,
        grid=(B, N_GROUPS),
        in_specs=[_group_spec(S, 0), _group_spec(S, 1), _group_spec(S, 2), _vt_spec(S),
                  pl.BlockSpec(brow.shape, lambda b, h: (0, 0))],
        out_specs=pl.BlockSpec((None, None, S, LANES), lambda b, h: (b, h, 0, 0)),
        out_shape=jax.ShapeDtypeStruct((B, N_GROUPS, S, LANES), jnp.bfloat16),
        scratch_shapes=[
            pltpu.VMEM((A_HEADS, A_BAND, TQ), jnp.float32),
            pltpu.VMEM((LANES + ONES_ROWS, S), jnp.bfloat16),
            pltpu.VMEM((2, A_BAND, 2 * TQ), jnp.float32),
            pltpu.VMEM((2, A_BAND, 2 * TQ), jnp.bfloat16),
        ],
        compiler_params=_ARB2,
        name="attn_a",
    )(qkg, qkg, qkg, vt, brow)


def _out_kernel(y_ref, x_ref, w_ref, g_ref, o_ref):
    y = jnp.concatenate([y_ref[gi] for gi in range(N_GROUPS)], axis=1)
    yo = jnp.dot(y, w_ref[...], preferred_element_type=jnp.float32)
    o_ref[...] = x_ref[...] + yo * _rms_scale(yo) * g_ref[...]


def _out_proj(y, x, w, gain, name):
    B, S, _ = x.shape
    return pl.pallas_call(
        _out_kernel,
        grid=(B, S // TM),
        in_specs=[
            pl.BlockSpec((None, N_GROUPS, TM, LANES), lambda b, i: (b, 0, i, 0)),
            pl.BlockSpec((None, TM, D_MODEL), lambda b, i: (b, i, 0)),
            pl.BlockSpec((D_MODEL, D_MODEL), lambda b, i: (0, 0)),
            pl.BlockSpec((1, D_MODEL), lambda b, i: (0, 0)),
        ],
        out_specs=pl.BlockSpec((None, TM, D_MODEL), lambda b, i: (b, i, 0)),
        out_shape=jax.ShapeDtypeStruct((B, S, D_MODEL), jnp.float32),
        compiler_params=_ARB2,
        name=name,
    )(y, x, w, gain)


def _rope(t, cos, sin_lo, sin_hi):
    return t * cos + pltpu.roll(t, LANES - ROT_DIM // 2, 1) * sin_lo + pltpu.roll(t, ROT_DIM // 2, 1) * sin_hi


def _proj_b_kernel(x_ref, pos_ref, freq_ref, gkv_ref, gb_ref, wk_ref, wvt_ref, wb_ref, o_ref, vt_ref):
    x = x_ref[...]
    xn = x * _rms_scale(x)
    u_kv = (xn * gkv_ref[...]).astype(jnp.bfloat16)
    u_b = (xn * gb_ref[...]).astype(jnp.bfloat16)

    ang = pos_ref[...].astype(jnp.float32) * freq_ref[...]
    cos = jnp.cos(ang)
    sin = jnp.sin(ang)
    lane = lax.broadcasted_iota(jnp.int32, ang.shape, 1) % HEAD_DIM
    sin_lo = jnp.where(lane < ROT_DIM // 2, -sin, 0.0)
    sin_hi = jnp.where((lane >= ROT_DIM // 2) & (lane < ROT_DIM), sin, 0.0)

    def rope_store(base, r, scale):
        for gi in range(N_GROUPS):
            t = _rope(r[:, gi * LANES:(gi + 1) * LANES], cos, sin_lo, sin_hi)
            if scale != 1.0:
                t = t * scale
            o_ref[base + gi] = t.astype(o_ref.dtype)

    k = jnp.dot(u_kv, wk_ref[...], preferred_element_type=jnp.float32)
    rope_store(0, k, 1.0)
    q = jnp.dot(u_b, wb_ref[:, :D_MODEL], preferred_element_type=jnp.float32)
    rope_store(N_GROUPS, q, Q_SCALE)
    g = jnp.dot(u_b, wb_ref[:, D_MODEL:], preferred_element_type=jnp.float32)
    _store_groups(o_ref, 2 * N_GROUPS, g)
    _store_transposed(vt_ref, wvt_ref, u_kv)


def _proj_b(h, pos, freq, gkv, gb, wk, wvt, wb):
    B, S, _ = h.shape
    return pl.pallas_call(
        _proj_b_kernel,
        grid=(B, S // TM),
        in_specs=[
            pl.BlockSpec((None, TM, D_MODEL), lambda b, i: (b, i, 0)),
            pl.BlockSpec((None, TM, 1), lambda b, i: (b, i, 0)),
            pl.BlockSpec((1, LANES), lambda b, i: (0, 0)),
            pl.BlockSpec((1, D_MODEL), lambda b, i: (0, 0)),
            pl.BlockSpec((1, D_MODEL), lambda b, i: (0, 0)),
            pl.BlockSpec((D_MODEL, D_MODEL), lambda b, i: (0, 0)),
            pl.BlockSpec((D_MODEL, D_MODEL), lambda b, i: (0, 0)),
            pl.BlockSpec((D_MODEL, 2 * D_MODEL), lambda b, i: (0, 0)),
        ],
        out_specs=[
            pl.BlockSpec((None, 3 * N_GROUPS, TM, LANES), lambda b, i: (b, 0, i, 0)),
            pl.BlockSpec((None, N_GROUPS, LANES, TM), lambda b, i: (b, 0, 0, i)),
        ],
        out_shape=[
            jax.ShapeDtypeStruct((B, 3 * N_GROUPS, S, LANES), jnp.bfloat16),
            jax.ShapeDtypeStruct((B, N_GROUPS, LANES, S), jnp.bfloat16),
        ],
        compiler_params=_ARB2,
        name="proj_b",
    )(h, pos, freq, gkv, gb, wk, wvt, wb)


def _attn_b_kernel(k_ref, q_ref, g_ref, vt_ref, lam_ref, subln_ref, y_ref, vx_scr, s_scr, p_scr):
    S = q_ref.shape[0]
    n_q = S // TQ
    lp = lam_ref[...]
    lam = (jnp.exp(jnp.sum(lp[0:1] * lp[1:2], axis=1, keepdims=True))
           - jnp.exp(jnp.sum(lp[2:3] * lp[3:4], axis=1, keepdims=True)) + LAM_INIT_B)

    vx_scr[:LANES, :] = vt_ref[...]
    vx_scr[LANES:, :] = jnp.ones((ONES_ROWS, S), vx_scr.dtype)

    low = lax.broadcasted_iota(jnp.int32, (TQ, LANES), 1) < HEAD_DIM
    k_chunk = lax.broadcasted_iota(jnp.int32, (TQ, 2 * TQ), 0) // CHUNK
    q_chunk = lax.broadcasted_iota(jnp.int32, (TQ, 2 * TQ), 1) % TQ // CHUNK
    diag_mask = k_chunk <= q_chunk

    def scores(qi):
        rows = slice(qi * TQ, (qi + 1) * TQ)
        qs = _stack_pair(q_ref[rows, :], low)
        if qi > 0:
            s_scr[qi % 2, :qi * TQ, :] = lax.dot_general(k_ref[:qi * TQ, :], qs, _NT,
                                                         preferred_element_type=jnp.float32)
        s_d = lax.dot_general(k_ref[rows, :], qs, _NT, preferred_element_type=jnp.float32)
        s_scr[qi % 2, rows, :] = jnp.where(diag_mask, s_d, NEG_INF)

    def softmax(qi):
        nk = (qi + 1) * TQ
        s = s_scr[qi % 2, :nk, :]
        m = jnp.max(s, axis=0, keepdims=True)
        p_scr[qi % 2, :nk, :] = jnp.exp2(s - m).astype(p_scr.dtype)

    def values(qi):
        nk = (qi + 1) * TQ
        rows = slice(qi * TQ, (qi + 1) * TQ)
        ol = jnp.dot(vx_scr[:, :nk], p_scr[qi % 2, :nk, :], preferred_element_type=jnp.float32)
        o_t = (ol[:LANES, :TQ] / ol[LANES:LANES + 1, :TQ]
               - lam * (ol[:LANES, TQ:] / ol[LANES:LANES + 1, TQ:]))
        o_t = o_t * lax.rsqrt(jnp.mean(o_t * o_t, axis=0, keepdims=True) + RMS_EPS)
        o = o_t.T * subln_ref[...] * (1.0 - LAM_INIT_B)
        gate = g_ref[rows, :].astype(jnp.float32)
        y_ref[rows, :] = (o * (gate * _sigmoid(gate))).astype(y_ref.dtype)

    order = list(range(n_q - 1, -1, -1))
    scores(order[0])
    for n, qi in enumerate(order):
        if n + 1 < n_q:
            scores(order[n + 1])
        softmax(qi)
        values(qi)


def _attn_b(kqg, vt, lam_rows, subln):
    B, _, S, _ = kqg.shape
    return pl.pallas_call(
        _attn_b_kernel,
        grid=(B, B_HEADS),
        in_specs=[_group_spec(S, 0), _group_spec(S, 1), _group_spec(S, 2), _vt_spec(S),
                  pl.BlockSpec(lam_rows.shape, lambda b, h: (0, 0)),
                  pl.BlockSpec((1, LANES), lambda b, h: (0, 0))],
        out_specs=pl.BlockSpec((None, None, S, LANES), lambda b, h: (b, h, 0, 0)),
        out_shape=jax.ShapeDtypeStruct((B, B_HEADS, S, LANES), jnp.bfloat16),
        scratch_shapes=[
            pltpu.VMEM((LANES + ONES_ROWS, S), jnp.bfloat16),
            pltpu.VMEM((2, S, 2 * TQ), jnp.float32),
            pltpu.VMEM((2, S, 2 * TQ), jnp.bfloat16),
        ],
        compiler_params=_ARB2,
        name="attn_b",
    )(kqg, kqg, kqg, vt, lam_rows, subln)


def kernel(x, positions, a_norm_pre, a_w_in, a_rel_bias, a_w_out, a_norm_post, kv_norm, kv_w,
           b_norm_pre, b_w_in, b_lambda_q1, b_lambda_k1, b_lambda_q2, b_lambda_k2, b_subln, b_w_out,
           b_norm_post):
    assert a_w_in.shape[0] == 1 and b_w_in.shape[0] == 1
    bf16 = jnp.bfloat16

    xs = np.arange(A_BAND + TQ)
    idx = np.clip(xs - TQ, -MAX_REL, MAX_REL) + MAX_REL
    brow = jnp.take(a_rel_bias[0], jnp.asarray(idx, jnp.int32), axis=1)

    half = ROT_DIM // 2
    inv_freq = jnp.power(jnp.float32(ROPE_THETA), -jnp.arange(half, dtype=jnp.float32) * 2.0 / ROT_DIM)
    head_freq = jnp.concatenate([inv_freq, inv_freq, jnp.zeros((HEAD_DIM - ROT_DIM,), jnp.float32)])
    freq = jnp.tile(head_freq, LANES // HEAD_DIM)[None, :]

    lam_rows = jnp.zeros((8, LANES), jnp.float32)
    lam_rows = lam_rows.at[0:4, :HEAD_DIM].set(
        jnp.stack([b_lambda_q1[0], b_lambda_k1[0], b_lambda_q2[0], b_lambda_k2[0]]).astype(jnp.float32))

    wa = a_w_in[0]
    w_qkg = jnp.concatenate([wa[:, :2 * D_MODEL], wa[:, 3 * D_MODEL:]], axis=1).astype(bf16)
    w_vt_a = wa[:, 2 * D_MODEL:3 * D_MODEL].T.astype(bf16)
    qkg, vt_a = _proj_a(x, a_norm_pre[0][None, :], w_qkg, w_vt_a)
    y_a = _attn_a(qkg, vt_a, brow)
    h1 = _out_proj(y_a, x, a_w_out[0].astype(bf16), a_norm_post[0][None, :], "out_a")

    kqg, vt_b = _proj_b(h1, positions[..., None], freq, kv_norm[None, :], b_norm_pre[0][None, :],
                        kv_w[:, :D_MODEL].astype(bf16), kv_w[:, D_MODEL:].T.astype(bf16),
                        b_w_in[0].astype(bf16))
    y_b = _attn_b(kqg, vt_b, lam_rows, b_subln[0][None, :])
    return _out_proj(y_b, h1, b_w_out[0].astype(bf16), b_norm_post[0][None, :], "out_b")
```

```python
import math

import jax
import jax.numpy as jnp
import numpy as np
from jax import lax
from jax.experimental import pallas as pl
from jax.experimental.pallas import tpu as pltpu

D_MODEL = 1024
CHUNK = 64
PAST_CHUNKS = 8
HEAD_DIM = 64
A_HEADS = 16
B_HEADS = 8
MAX_REL = 128
ROT_DIM = 16
ROPE_THETA = 500000.0
RMS_EPS = 1e-6
NEG_INF = -1e30
LAM_INIT_B = 0.8 - 0.6 * math.exp(-0.3 * 1)
LOG2E = math.log2(math.e)
Q_SCALE = LOG2E / math.sqrt(HEAD_DIM)

LANES = 128
N_GROUPS = D_MODEL // LANES
TM = 512
TQ = 256
A_BAND = PAST_CHUNKS * CHUNK + TQ
VMEM_LIMIT = 56 * 1024 * 1024

_NT = (((1,), (1,)), ((), ()))
_ARB2 = pltpu.CompilerParams(dimension_semantics=("arbitrary", "arbitrary"),
                             vmem_limit_bytes=VMEM_LIMIT)


def _rms_scale(x):
    return lax.rsqrt(jnp.mean(x * x, axis=-1, keepdims=True) + RMS_EPS)


def _sigmoid(x):
    return 1.0 / (1.0 + jnp.exp(-x))


def _store_groups(o_ref, base, r):
    for gi in range(N_GROUPS):
        o_ref[base + gi] = r[:, gi * LANES:(gi + 1) * LANES].astype(o_ref.dtype)


def _group_spec(S, slab):
    return pl.BlockSpec((None, None, S, LANES), lambda b, h: (b, slab * N_GROUPS + h, 0, 0))


def _proj_a_kernel(x_ref, g_ref, w_ref, o_ref):
    x = x_ref[...]
    y = (x * _rms_scale(x) * g_ref[...]).astype(jnp.bfloat16)
    for c in range(4):
        r = jnp.dot(y, w_ref[:, c * D_MODEL:(c + 1) * D_MODEL], preferred_element_type=jnp.float32)
        if c == 0:
            r = r * Q_SCALE
        _store_groups(o_ref, c * N_GROUPS, r)


def _proj_a(x, gain, w):
    B, S, _ = x.shape
    return pl.pallas_call(
        _proj_a_kernel,
        grid=(B, S // TM),
        in_specs=[
            pl.BlockSpec((None, TM, D_MODEL), lambda b, i: (b, i, 0)),
            pl.BlockSpec((1, D_MODEL), lambda b, i: (0, 0)),
            pl.BlockSpec((D_MODEL, 4 * D_MODEL), lambda b, i: (0, 0)),
        ],
        out_specs=pl.BlockSpec((None, 4 * N_GROUPS, TM, LANES), lambda b, i: (b, 0, i, 0)),
        out_shape=jax.ShapeDtypeStruct((B, 4 * N_GROUPS, S, LANES), jnp.bfloat16),
        compiler_params=_ARB2,
        name="proj_a",
    )(x, gain, w)


def _attn_a_kernel(q_ref, k_ref, v_ref, g_ref, brow_ref, y_ref, bias_scr, vx_scr, s_scr, p_scr):
    b = pl.program_id(0)
    hp = pl.program_id(1)
    S = q_ref.shape[0]

    @pl.when((b == 0) & (hp == 0))
    def _build_bias():
        r_chunk = lax.broadcasted_iota(jnp.int32, (TQ, A_BAND), 0) // CHUNK
        j_chunk = lax.broadcasted_iota(jnp.int32, (TQ, A_BAND), 1) // CHUNK
        in_band = (j_chunk >= r_chunk) & (j_chunk <= r_chunk + PAST_CHUNKS)
        for h in range(A_HEADS):
            row = jnp.broadcast_to(brow_ref[h:h + 1, :], (TQ, brow_ref.shape[1]))
            toep = pltpu.roll(row, 0, 1, stride=1, stride_axis=0)
            bias_scr[h] = jnp.where(in_band, toep[:, TQ:] * LOG2E, NEG_INF)

    vx_scr[:, :LANES] = v_ref[...]
    vx_scr[:, LANES:] = jnp.ones((S, LANES), vx_scr.dtype)

    low = lax.broadcasted_iota(jnp.int32, (TQ, LANES), 1) < HEAD_DIM

    def keys_of(qi):
        return slice(max(0, (qi + 1) * TQ - A_BAND), (qi + 1) * TQ)

    def scores(n):
        qi, half = divmod(n, 2)
        keys = keys_of(qi)
        nk = keys.stop - keys.start
        q2 = q_ref[qi * TQ:(qi + 1) * TQ, :]
        qm = jnp.where(low if half == 0 else ~low, q2, jnp.zeros_like(q2))
        s = lax.dot_general(qm, k_ref[keys, :], _NT, preferred_element_type=jnp.float32)
        s_scr[n, :, :nk] = s + bias_scr[2 * hp + half, :, A_BAND - nk:]

    def softmax(n):
        keys = keys_of(n // 2)
        nk = keys.stop - keys.start
        s = s_scr[n, :, :nk]
        m = jnp.max(s, axis=1, keepdims=True)
        p_scr[n, :, :nk] = jnp.exp2(s - m).astype(p_scr.dtype)

    def values(n):
        keys = keys_of(n // 2)
        nk = keys.stop - keys.start
        ol = jnp.dot(p_scr[n, :, :nk], vx_scr[keys, :], preferred_element_type=jnp.float32)
        return ol[:, :LANES] / ol[:, LANES:]

    n_stages = 2 * (S // TQ)
    order = list(range(n_stages - 1, -1, -1))
    scores(order[0])
    outs = {}
    for i, n in enumerate(order):
        if i + 1 < n_stages:
            scores(order[i + 1])
        softmax(n)
        outs[n] = values(n)
        if n % 2 == 0:
            rows = slice((n // 2) * TQ, (n // 2 + 1) * TQ)
            o_pair = jnp.where(low, outs.pop(n), outs.pop(n + 1))
            gate = g_ref[rows, :].astype(jnp.float32)
            y_ref[rows, :] = (o_pair * (gate * _sigmoid(gate))).astype(y_ref.dtype)


def _attn_a(qkvg, brow):
    B, _, S, _ = qkvg.shape
    return pl.pallas_call(
        _attn_a_kernel,
        grid=(B, N_GROUPS),
        in_specs=[_group_spec(S, 0), _group_spec(S, 1), _group_spec(S, 2), _group_spec(S, 3),
                  pl.BlockSpec(brow.shape, lambda b, h: (0, 0))],
        out_specs=pl.BlockSpec((None, None, S, LANES), lambda b, h: (b, h, 0, 0)),
        out_shape=jax.ShapeDtypeStruct((B, N_GROUPS, S, LANES), jnp.bfloat16),
        scratch_shapes=[
            pltpu.VMEM((A_HEADS, TQ, A_BAND), jnp.float32),
            pltpu.VMEM((S, 2 * LANES), jnp.bfloat16),
            pltpu.VMEM((2 * (S // TQ), TQ, A_BAND), jnp.float32),
            pltpu.VMEM((2 * (S // TQ), TQ, A_BAND), jnp.bfloat16),
        ],
        compiler_params=_ARB2,
        name="attn_a",
    )(qkvg, qkvg, qkvg, qkvg, brow)


def _out_kernel(y_ref, x_ref, w_ref, g_ref, o_ref):
    y = jnp.concatenate([y_ref[gi] for gi in range(N_GROUPS)], axis=1)
    yo = jnp.dot(y, w_ref[...], preferred_element_type=jnp.float32)
    o_ref[...] = x_ref[...] + yo * _rms_scale(yo) * g_ref[...]


def _out_proj(y, x, w, gain, name):
    B, S, _ = x.shape
    return pl.pallas_call(
        _out_kernel,
        grid=(B, S // TM),
        in_specs=[
            pl.BlockSpec((None, N_GROUPS, TM, LANES), lambda b, i: (b, 0, i, 0)),
            pl.BlockSpec((None, TM, D_MODEL), lambda b, i: (b, i, 0)),
            pl.BlockSpec((D_MODEL, D_MODEL), lambda b, i: (0, 0)),
            pl.BlockSpec((1, D_MODEL), lambda b, i: (0, 0)),
        ],
        out_specs=pl.BlockSpec((None, TM, D_MODEL), lambda b, i: (b, i, 0)),
        out_shape=jax.ShapeDtypeStruct((B, S, D_MODEL), jnp.float32),
        compiler_params=_ARB2,
        name=name,
    )(y, x, w, gain)


def _rope(t, cos, sin_lo, sin_hi):
    return t * cos + pltpu.roll(t, LANES - ROT_DIM // 2, 1) * sin_lo + pltpu.roll(t, ROT_DIM // 2, 1) * sin_hi


def _proj_b_kernel(y_ref, xin_ref, wout_ref, gpost_ref, pos_ref, freq_ref, gkv_ref, gb_ref, wkv_ref, wb_ref,
                   h_ref, o_ref):
    y = jnp.concatenate([y_ref[gi] for gi in range(N_GROUPS)], axis=1)
    yo = jnp.dot(y, wout_ref[...], preferred_element_type=jnp.float32)
    x = xin_ref[...] + yo * _rms_scale(yo) * gpost_ref[...]
    h_ref[...] = x
    xn = x * _rms_scale(x)
    u_kv = (xn * gkv_ref[...]).astype(jnp.bfloat16)
    u_b = (xn * gb_ref[...]).astype(jnp.bfloat16)

    ang = pos_ref[...].astype(jnp.float32) * freq_ref[...]
    cos = jnp.cos(ang)
    sin = jnp.sin(ang)
    lane = lax.broadcasted_iota(jnp.int32, ang.shape, 1) % HEAD_DIM
    sin_lo = jnp.where(lane < ROT_DIM // 2, -sin, 0.0)
    sin_hi = jnp.where((lane >= ROT_DIM // 2) & (lane < ROT_DIM), sin, 0.0)

    def rope_store(base, r, scale):
        for gi in range(N_GROUPS):
            t = _rope(r[:, gi * LANES:(gi + 1) * LANES], cos, sin_lo, sin_hi)
            if scale != 1.0:
                t = t * scale
            o_ref[base + gi] = t.astype(o_ref.dtype)

    k = jnp.dot(u_kv, wkv_ref[:, :D_MODEL], preferred_element_type=jnp.float32)
    rope_store(0, k, 1.0)
    v = jnp.dot(u_kv, wkv_ref[:, D_MODEL:], preferred_element_type=jnp.float32)
    _store_groups(o_ref, N_GROUPS, v)
    q = jnp.dot(u_b, wb_ref[:, :D_MODEL], preferred_element_type=jnp.float32)
    rope_store(2 * N_GROUPS, q, Q_SCALE)
    g = jnp.dot(u_b, wb_ref[:, D_MODEL:], preferred_element_type=jnp.float32)
    _store_groups(o_ref, 3 * N_GROUPS, g)


def _proj_b(y, x, wout, gpost, pos, freq, gkv, gb, wkv, wb):
    B, S, _ = x.shape
    return pl.pallas_call(
        _proj_b_kernel,
        grid=(B, S // TM),
        in_specs=[
            pl.BlockSpec((None, N_GROUPS, TM, LANES), lambda b, i: (b, 0, i, 0)),
            pl.BlockSpec((None, TM, D_MODEL), lambda b, i: (b, i, 0)),
            pl.BlockSpec((D_MODEL, D_MODEL), lambda b, i: (0, 0)),
            pl.BlockSpec((1, D_MODEL), lambda b, i: (0, 0)),
            pl.BlockSpec((None, TM, 1), lambda b, i: (b, i, 0)),
            pl.BlockSpec((1, LANES), lambda b, i: (0, 0)),
            pl.BlockSpec((1, D_MODEL), lambda b, i: (0, 0)),
            pl.BlockSpec((1, D_MODEL), lambda b, i: (0, 0)),
            pl.BlockSpec((D_MODEL, 2 * D_MODEL), lambda b, i: (0, 0)),
            pl.BlockSpec((D_MODEL, 2 * D_MODEL), lambda b, i: (0, 0)),
        ],
        out_specs=[
            pl.BlockSpec((None, TM, D_MODEL), lambda b, i: (b, i, 0)),
            pl.BlockSpec((None, 4 * N_GROUPS, TM, LANES), lambda b, i: (b, 0, i, 0)),
        ],
        out_shape=[
            jax.ShapeDtypeStruct((B, S, D_MODEL), jnp.float32),
            jax.ShapeDtypeStruct((B, 4 * N_GROUPS, S, LANES), jnp.bfloat16),
        ],
        compiler_params=_ARB2,
        name="proj_b",
    )(y, x, wout, gpost, pos, freq, gkv, gb, wkv, wb)


def _attn_b_kernel(k_ref, v_ref, q_ref, g_ref, lam_ref, subln_ref, y_ref, vx_scr, *stage_scr):
    S = q_ref.shape[0]
    n_q = S // TQ
    s_scr, p_scr = stage_scr[:n_q], stage_scr[n_q:]
    lp = lam_ref[...]
    lam = (jnp.exp(jnp.sum(lp[0:1] * lp[1:2], axis=1, keepdims=True))
           - jnp.exp(jnp.sum(lp[2:3] * lp[3:4], axis=1, keepdims=True)) + LAM_INIT_B)

    vx_scr[:, :LANES] = v_ref[...]
    vx_scr[:, LANES:] = jnp.ones((S, LANES), vx_scr.dtype)

    low =lax.broadcasted_iota(jnp.int32, (TQ, LANES), 1) < HEAD_DIM
    q_chunk = lax.broadcasted_iota(jnp.int32, (2 * TQ, TQ), 0) % TQ // CHUNK
    k_chunk = lax.broadcasted_iota(jnp.int32, (2 * TQ, TQ), 1) // CHUNK
    diag_mask = k_chunk <= q_chunk

    def scores(qi):
        q2 = q_ref[qi * TQ:(qi + 1) * TQ, :]
        zero = jnp.zeros_like(q2)
        qs = jnp.concatenate([jnp.where(low, q2, zero), jnp.where(low, zero, q2)], axis=0)
        if qi > 0:
            s_scr[qi][:, :qi * TQ] = lax.dot_general(qs, k_ref[:qi * TQ, :], _NT,
                                                     preferred_element_type=jnp.float32)
        s_d = lax.dot_general(qs, k_ref[qi * TQ:(qi + 1) * TQ, :], _NT, preferred_element_type=jnp.float32)
        s_scr[qi][:, qi * TQ:] = jnp.where(diag_mask, s_d, NEG_INF)

    def softmax(qi):
        s = s_scr[qi][...]
        m = jnp.max(s, axis=1, keepdims=True)
        p_scr[qi][...] = jnp.exp2(s - m).astype(p_scr[qi].dtype)

    def values(qi):
        nk = (qi + 1) * TQ
        rows = slice(qi * TQ, (qi + 1) * TQ)
        ol = jnp.dot(p_scr[qi][...], vx_scr[:nk, :], preferred_element_type=jnp.float32)
        o = ol[:TQ, :LANES] / ol[:TQ, LANES:] - lam * (ol[TQ:, :LANES] / ol[TQ:, LANES:])
        o = o * _rms_scale(o) * subln_ref[...] * (1.0 - LAM_INIT_B)
        gate = g_ref[rows, :].astype(jnp.float32)
        y_ref[rows, :] = (o * (gate * _sigmoid(gate))).astype(y_ref.dtype)

    order = list(range(n_q - 1, -1, -1))
    scores(order[0])
    for n, qi in enumerate(order):
        if n + 1 < n_q:
            scores(order[n + 1])
        softmax(qi)
        values(qi)


def _attn_b(kvqg, lam_rows, subln):
    B, _, S, _ = kvqg.shape
    n_q = S // TQ
    return pl.pallas_call(
        _attn_b_kernel,
        grid=(B, B_HEADS),
        in_specs=[_group_spec(S, 0), _group_spec(S, 1), _group_spec(S, 2), _group_spec(S, 3),
                  pl.BlockSpec(lam_rows.shape, lambda b, h: (0, 0)),
                  pl.BlockSpec((1, LANES), lambda b, h: (0, 0))],
        out_specs=pl.BlockSpec((None, None, S, LANES), lambda b, h: (b, h, 0, 0)),
        out_shape=jax.ShapeDtypeStruct((B, B_HEADS, S, LANES), jnp.bfloat16),
        scratch_shapes=(
            [pltpu.VMEM((S, 2 * LANES), jnp.bfloat16)]
            + [pltpu.VMEM((2 * TQ, (qi + 1) * TQ), jnp.float32) for qi in range(n_q)]
            + [pltpu.VMEM((2 * TQ, (qi + 1) * TQ), jnp.bfloat16) for qi in range(n_q)]),
        compiler_params=_ARB2,
        name="attn_b",
    )(kvqg, kvqg, kvqg, kvqg, lam_rows, subln)


def kernel(x, positions, a_norm_pre, a_w_in, a_rel_bias, a_w_out, a_norm_post, kv_norm, kv_w,
           b_norm_pre, b_w_in, b_lambda_q1, b_lambda_k1, b_lambda_q2, b_lambda_k2, b_subln, b_w_out,
           b_norm_post):
    assert a_w_in.shape[0] == 1 and b_w_in.shape[0] == 1
    bf16 = jnp.bfloat16

    m = np.arange(A_BAND + TQ)
    idx = np.clip(A_BAND - m, -MAX_REL, MAX_REL) + MAX_REL
    brow = jnp.take(a_rel_bias[0], jnp.asarray(idx, jnp.int32), axis=1)

    half = ROT_DIM // 2
    inv_freq = jnp.power(jnp.float32(ROPE_THETA), -jnp.arange(half, dtype=jnp.float32) * 2.0 / ROT_DIM)
    head_freq = jnp.concatenate([inv_freq, inv_freq, jnp.zeros((HEAD_DIM - ROT_DIM,), jnp.float32)])
    freq = jnp.tile(head_freq, LANES // HEAD_DIM)[None, :]

    lam_rows = jnp.zeros((8, LANES), jnp.float32)
    lam_rows = lam_rows.at[0:4, :HEAD_DIM].set(
        jnp.stack([b_lambda_q1[0], b_lambda_k1[0], b_lambda_q2[0], b_lambda_k2[0]]).astype(jnp.float32))

    qkvg = _proj_a(x, a_norm_pre[0][None, :], a_w_in[0].astype(bf16))
    y_a = _attn_a(qkvg, brow)
    h1, kvqg = _proj_b(y_a, x, a_w_out[0].astype(bf16), a_norm_post[0][None, :], positions[..., None], freq,
                       kv_norm[None, :], b_norm_pre[0][None, :], kv_w.astype(bf16), b_w_in[0].astype(bf16))
    y_b = _attn_b(kvqg, lam_rows, b_subln[0][None, :])
    return _out_proj(y_b, h1, b_w_out[0].astype(bf16), b_norm_post[0][None, :], "out_b")
```

```python
import math

import jax
import jax.numpy as jnp
import numpy as np
from jax import lax
from jax.experimental import pallas as pl
from jax.experimental.pallas import tpu as pltpu

D_MODEL = 1024
CHUNK = 64
PAST_CHUNKS = 8
HEAD_DIM = 64
A_HEADS = 16
B_HEADS = 8
MAX_REL = 128
ROT_DIM = 16
ROPE_THETA = 500000.0
RMS_EPS = 1e-6
NEG_INF = -1e30
LAM_INIT_B = 0.8 - 0.6 * math.exp(-0.3 * 1)
LOG2E = math.log2(math.e)
Q_SCALE = LOG2E / math.sqrt(HEAD_DIM)

LANES = 128
N_GROUPS = D_MODEL // LANES
TM = 512
TQ = 256
A_BAND = PAST_CHUNKS * CHUNK + TQ
VMEM_LIMIT = 56 * 1024 * 1024

_NT = (((1,), (1,)), ((), ()))
_ARB2 = pltpu.CompilerParams(dimension_semantics=("arbitrary", "arbitrary"),
                             vmem_limit_bytes=VMEM_LIMIT)


def _rms_scale(x):
    return lax.rsqrt(jnp.mean(x * x, axis=-1, keepdims=True) + RMS_EPS)


def _sigmoid(x):
    return 1.0 / (1.0 + jnp.exp(-x))


def _store_groups(o_ref, base, r):
    for gi in range(N_GROUPS):
        o_ref[base + gi] = r[:, gi * LANES:(gi + 1) * LANES].astype(o_ref.dtype)


def _group_spec(S, slab):
    return pl.BlockSpec((None, None, S, LANES), lambda b, h: (b, slab * N_GROUPS + h, 0, 0))


def _proj_a_kernel(x_ref, g_ref, w_ref, o_ref):
    x = x_ref[...]
    y = (x * _rms_scale(x) * g_ref[...]).astype(jnp.bfloat16)
    for c in range(4):
        r = jnp.dot(y, w_ref[:, c * D_MODEL:(c + 1) * D_MODEL], preferred_element_type=jnp.float32)
        if c == 0:
            r = r * Q_SCALE
        _store_groups(o_ref, c * N_GROUPS, r)


def _proj_a(x, gain, w):
    B, S, _ = x.shape
    return pl.pallas_call(
        _proj_a_kernel,
        grid=(B, S // TM),
        in_specs=[
            pl.BlockSpec((None, TM, D_MODEL), lambda b, i: (b, i, 0)),
            pl.BlockSpec((1, D_MODEL), lambda b, i: (0, 0)),
            pl.BlockSpec((D_MODEL, 4 * D_MODEL), lambda b, i: (0, 0)),
        ],
        out_specs=pl.BlockSpec((None, 4 * N_GROUPS, TM, LANES), lambda b, i: (b, 0, i, 0)),
        out_shape=jax.ShapeDtypeStruct((B, 4 * N_GROUPS, S, LANES), jnp.bfloat16),
        compiler_params=_ARB2,
        name="proj_a",
    )(x, gain, w)


def _attn_a_kernel(q_ref, k_ref, v_ref, g_ref, brow_ref, y_ref, bias_scr, vx_scr, s_scr, p_scr):
    b = pl.program_id(0)
    hp = pl.program_id(1)
    S = q_ref.shape[0]

    @pl.when((b == 0) & (hp == 0))
    def _build_bias():
        r_chunk = lax.broadcasted_iota(jnp.int32, (TQ, A_BAND), 0) // CHUNK
        j_chunk = lax.broadcasted_iota(jnp.int32, (TQ, A_BAND), 1) // CHUNK
        in_band = (j_chunk >= r_chunk) & (j_chunk <= r_chunk + PAST_CHUNKS)
        for h in range(A_HEADS):
            row = jnp.broadcast_to(brow_ref[h:h + 1, :], (TQ, brow_ref.shape[1]))
            toep = pltpu.roll(row, 0, 1, stride=1, stride_axis=0)
            bias_scr[h] = jnp.where(in_band, toep[:, TQ:] * LOG2E, NEG_INF)

    vx_scr[:, :LANES] = v_ref[...]
    vx_scr[:, LANES:] = jnp.ones((S, LANES), vx_scr.dtype)

    low = lax.broadcasted_iota(jnp.int32, (TQ, LANES), 1) < HEAD_DIM

    def keys_of(qi):
        return slice(max(0, (qi + 1) * TQ - A_BAND), (qi + 1) * TQ)

    def scores(n):
        qi, half = divmod(n, 2)
        keys = keys_of(qi)
        nk = keys.stop - keys.start
        q2 = q_ref[qi * TQ:(qi + 1) * TQ, :]
        qm = jnp.where(low if half == 0 else ~low, q2, jnp.zeros_like(q2))
        s = lax.dot_general(qm, k_ref[keys, :], _NT, preferred_element_type=jnp.float32)
        s_scr[n, :, :nk] = s + bias_scr[2 * hp + half, :, A_BAND - nk:]

    def softmax(n):
        keys = keys_of(n // 2)
        nk = keys.stop - keys.start
        s = s_scr[n, :, :nk]
        m = jnp.max(s, axis=1, keepdims=True)
        p_scr[n, :, :nk] = jnp.exp2(s - m).astype(p_scr.dtype)

    def values(n):
        keys = keys_of(n // 2)
        nk = keys.stop - keys.start
        ol = jnp.dot(p_scr[n, :, :nk], vx_scr[keys, :], preferred_element_type=jnp.float32)
        return ol[:, :LANES] / ol[:, LANES:]

    n_stages = 2 * (S // TQ)
    order = list(range(n_stages - 1, -1, -1))
    outs = {}

    def finish(n):
        outs[n] = values(n)
        if n % 2 == 0:
            rows = slice((n // 2) * TQ, (n // 2 + 1) * TQ)
            o_pair = jnp.where(low, outs.pop(n), outs.pop(n + 1))
            gate = g_ref[rows, :].astype(jnp.float32)
            y_ref[rows, :] = (o_pair * (gate * _sigmoid(gate))).astype(y_ref.dtype)

    scores(order[0])
    for i, n in enumerate(order):
        if i + 1 < n_stages:
            scores(order[i + 1])
        softmax(n)
        if i > 0:
            finish(order[i - 1])
    finish(order[-1])


def _attn_a(qkvg, brow):
    B, _, S, _ = qkvg.shape
    return pl.pallas_call(
        _attn_a_kernel,
        grid=(B, N_GROUPS),
        in_specs=[_group_spec(S, 0), _group_spec(S, 1), _group_spec(S, 2), _group_spec(S, 3),
                  pl.BlockSpec(brow.shape, lambda b, h: (0, 0))],
        out_specs=pl.BlockSpec((None, None, S, LANES), lambda b, h: (b, h, 0, 0)),
        out_shape=jax.ShapeDtypeStruct((B, N_GROUPS, S, LANES), jnp.bfloat16),
        scratch_shapes=[
            pltpu.VMEM((A_HEADS, TQ, A_BAND), jnp.float32),
            pltpu.VMEM((S, 2 * LANES), jnp.bfloat16),
            pltpu.VMEM((2 * (S // TQ), TQ, A_BAND), jnp.float32),
            pltpu.VMEM((2 * (S // TQ), TQ, A_BAND), jnp.bfloat16),
        ],
        compiler_params=_ARB2,
        name="attn_a",
    )(qkvg, qkvg, qkvg, qkvg, brow)


def _out_kernel(y_ref, x_ref, w_ref, g_ref, o_ref):
    y = jnp.concatenate([y_ref[gi] for gi in range(N_GROUPS)], axis=1)
    yo = jnp.dot(y, w_ref[...], preferred_element_type=jnp.float32)
    o_ref[...] = x_ref[...] + yo * _rms_scale(yo) * g_ref[...]


def _out_proj(y, x, w, gain, name):
    B, S, _ = x.shape
    return pl.pallas_call(
        _out_kernel,
        grid=(B, S // TM),
        in_specs=[
            pl.BlockSpec((None, N_GROUPS, TM, LANES), lambda b, i: (b, 0, i, 0)),
            pl.BlockSpec((None, TM, D_MODEL), lambda b, i: (b, i, 0)),
            pl.BlockSpec((D_MODEL, D_MODEL), lambda b, i: (0, 0)),
            pl.BlockSpec((1, D_MODEL), lambda b, i: (0, 0)),
        ],
        out_specs=pl.BlockSpec((None, TM, D_MODEL), lambda b, i: (b, i, 0)),
        out_shape=jax.ShapeDtypeStruct((B, S, D_MODEL), jnp.float32),
        compiler_params=_ARB2,
        name=name,
    )(y, x, w, gain)


def _rope(t, cos, sin_lo, sin_hi):
    return t * cos + pltpu.roll(t, LANES - ROT_DIM // 2, 1) * sin_lo + pltpu.roll(t, ROT_DIM // 2, 1) * sin_hi


def _proj_b_kernel(y_ref, xin_ref, wout_ref, gpost_ref, pos_ref, freq_ref, cosr_ref, sinr_ref, gkv_ref, gb_ref,
                   wkv_ref, wb_ref, h_ref, o_ref):
    y = jnp.concatenate([y_ref[gi] for gi in range(N_GROUPS)], axis=1)
    yo = jnp.dot(y, wout_ref[...], preferred_element_type=jnp.float32)
    x = xin_ref[...] + yo * _rms_scale(yo) * gpost_ref[...]
    h_ref[...] = x
    xn = x * _rms_scale(x)
    u_kv = (xn * gkv_ref[...]).astype(jnp.bfloat16)
    u_b = (xn * gb_ref[...]).astype(jnp.bfloat16)

    ang0 = pos_ref[0:1, :].astype(jnp.float32) * freq_ref[...]
    cos0 = jnp.cos(ang0)
    sin0 = jnp.sin(ang0)
    cos = cos0 * cosr_ref[...] - sin0 * sinr_ref[...]
    sin = sin0 * cosr_ref[...] + cos0 * sinr_ref[...]
    lane = lax.broadcasted_iota(jnp.int32, cos.shape, 1) % HEAD_DIM
    sin_lo = jnp.where(lane < ROT_DIM // 2, -sin, 0.0)
    sin_hi = jnp.where((lane >= ROT_DIM // 2) & (lane < ROT_DIM), sin, 0.0)

    def rope_store(base, r, scale):
        for gi in range(N_GROUPS):
            t = _rope(r[:, gi * LANES:(gi + 1) * LANES], cos, sin_lo, sin_hi)
            if scale != 1.0:
                t = t * scale
            o_ref[base + gi] = t.astype(o_ref.dtype)

    k = jnp.dot(u_kv, wkv_ref[:, :D_MODEL], preferred_element_type=jnp.float32)
    rope_store(0, k, 1.0)
    v = jnp.dot(u_kv, wkv_ref[:, D_MODEL:], preferred_element_type=jnp.float32)
    _store_groups(o_ref, N_GROUPS, v)
    q = jnp.dot(u_b, wb_ref[:, :D_MODEL], preferred_element_type=jnp.float32)
    rope_store(2 * N_GROUPS, q, Q_SCALE)
    g = jnp.dot(u_b, wb_ref[:, D_MODEL:], preferred_element_type=jnp.float32)
    _store_groups(o_ref, 3 * N_GROUPS, g)


def _proj_b(y, x, wout, gpost, pos, freq, gkv, gb, wkv, wb):
    B, S, _ = x.shape
    row_ang = jnp.arange(TM, dtype=jnp.float32)[:, None] * freq
    cosr, sinr = jnp.cos(row_ang), jnp.sin(row_ang)
    return pl.pallas_call(
        _proj_b_kernel,
        grid=(B, S // TM),
        in_specs=[
            pl.BlockSpec((None, N_GROUPS, TM, LANES), lambda b, i: (b, 0, i, 0)),
            pl.BlockSpec((None, TM, D_MODEL), lambda b, i: (b, i, 0)),
            pl.BlockSpec((D_MODEL, D_MODEL), lambda b, i: (0, 0)),
            pl.BlockSpec((1, D_MODEL), lambda b, i: (0, 0)),
            pl.BlockSpec((None, TM, 1), lambda b, i: (b, i, 0)),
            pl.BlockSpec((1, LANES), lambda b, i: (0, 0)),
            pl.BlockSpec((TM, LANES), lambda b, i: (0, 0)),
            pl.BlockSpec((TM, LANES), lambda b, i: (0, 0)),
            pl.BlockSpec((1, D_MODEL), lambda b, i: (0, 0)),
            pl.BlockSpec((1, D_MODEL), lambda b, i: (0, 0)),
            pl.BlockSpec((D_MODEL, 2 * D_MODEL), lambda b, i: (0, 0)),
            pl.BlockSpec((D_MODEL, 2 * D_MODEL), lambda b, i: (0, 0)),
        ],
        out_specs=[
            pl.BlockSpec((None, TM, D_MODEL), lambda b, i: (b, i, 0)),
            pl.BlockSpec((None, 4 * N_GROUPS, TM, LANES), lambda b, i: (b, 0, i, 0)),
        ],
        out_shape=[
            jax.ShapeDtypeStruct((B, S, D_MODEL), jnp.float32),
            jax.ShapeDtypeStruct((B, 4 * N_GROUPS, S, LANES), jnp.bfloat16),
        ],
        compiler_params=_ARB2,
        name="proj_b",
    )(y, x, wout, gpost, pos, freq, cosr, sinr, gkv, gb, wkv, wb)


def _attn_b_kernel(k_ref, v_ref, q_ref, g_ref, lam_ref, subln_ref, y_ref, vx_scr, *stage_scr):
    S = q_ref.shape[0]
    n_q = S // TQ
    s_scr, p_scr = stage_scr[:n_q], stage_scr[n_q:]
    lp = lam_ref[...]
    lam = (jnp.exp(jnp.sum(lp[0:1] * lp[1:2], axis=1, keepdims=True))
           - jnp.exp(jnp.sum(lp[2:3] * lp[3:4], axis=1, keepdims=True)) + LAM_INIT_B)

    vx_scr[:, :LANES] = v_ref[...]
    vx_scr[:, LANES:] = jnp.ones((S, LANES), vx_scr.dtype)

    low = lax.broadcasted_iota(jnp.int32, (TQ, LANES), 1) < HEAD_DIM
    q_chunk = lax.broadcasted_iota(jnp.int32, (2 * TQ, TQ), 0) % TQ // CHUNK
    k_chunk = lax.broadcasted_iota(jnp.int32, (2 * TQ, TQ), 1) // CHUNK
    diag_mask = k_chunk <= q_chunk

    def scores(qi):
        q2 = q_ref[qi * TQ:(qi + 1) * TQ, :]
        zero = jnp.zeros_like(q2)
        qs = jnp.concatenate([jnp.where(low, q2, zero), jnp.where(low, zero, q2)], axis=0)
        if qi > 0:
            s_scr[qi][:, :qi * TQ] = lax.dot_general(qs, k_ref[:qi * TQ, :], _NT,
                                                     preferred_element_type=jnp.float32)
        s_d = lax.dot_general(qs, k_ref[qi * TQ:(qi + 1) * TQ, :], _NT, preferred_element_type=jnp.float32)
        s_scr[qi][:, qi * TQ:] = jnp.where(diag_mask, s_d, NEG_INF)

    def softmax(qi):
        s = s_scr[qi][...]
        m = jnp.max(s, axis=1, keepdims=True)
        p_scr[qi][...] = jnp.exp2(s - m).astype(p_scr[qi].dtype)

    def values(qi):
        nk = (qi + 1) * TQ
        rows = slice(qi * TQ, (qi + 1) * TQ)
        ol = jnp.dot(p_scr[qi][...], vx_scr[:nk, :], preferred_element_type=jnp.float32)
        o = ol[:TQ, :LANES] / ol[:TQ, LANES:] - lam * (ol[TQ:, :LANES] / ol[TQ:, LANES:])
        o = o * _rms_scale(o) * subln_ref[...] * (1.0 - LAM_INIT_B)
        gate = g_ref[rows, :].astype(jnp.float32)
        y_ref[rows, :] = (o * (gate * _sigmoid(gate))).astype(y_ref.dtype)

    order = list(range(n_q - 1, -1, -1))
    scores(order[0])
    for n, qi in enumerate(order):
        if n + 1 < n_q:
            scores(order[n + 1])
        softmax(qi)
        if n > 0:
            values(order[n - 1])
    values(order[-1])


def _attn_b(kvqg, lam_rows, subln):
    B, _, S, _ = kvqg.shape
    n_q = S // TQ
    return pl.pallas_call(
        _attn_b_kernel,
        grid=(B, B_HEADS),
        in_specs=[_group_spec(S, 0), _group_spec(S, 1), _group_spec(S, 2), _group_spec(S, 3),
                  pl.BlockSpec(lam_rows.shape, lambda b, h: (0, 0)),
                  pl.BlockSpec((1, LANES), lambda b, h: (0, 0))],
        out_specs=pl.BlockSpec((None, None, S, LANES), lambda b, h: (b, h, 0, 0)),
        out_shape=jax.ShapeDtypeStruct((B, B_HEADS, S, LANES), jnp.bfloat16),
        scratch_shapes=(
            [pltpu.VMEM((S, 2 * LANES), jnp.bfloat16)]
            + [pltpu.VMEM((2 * TQ, (qi + 1) * TQ), jnp.float32) for qi in range(n_q)]
            + [pltpu.VMEM((2 * TQ, (qi + 1) * TQ), jnp.bfloat16) for qi in range(n_q)]),
        compiler_params=_ARB2,
        name="attn_b",
    )(kvqg, kvqg, kvqg, kvqg, lam_rows, subln)


def kernel(x, positions, a_norm_pre, a_w_in, a_rel_bias, a_w_out, a_norm_post, kv_norm, kv_w,
           b_norm_pre, b_w_in, b_lambda_q1, b_lambda_k1, b_lambda_q2, b_lambda_k2, b_subln, b_w_out,
           b_norm_post):
    assert a_w_in.shape[0] == 1 and b_w_in.shape[0] == 1
    bf16 = jnp.bfloat16

    m = np.arange(A_BAND + TQ)
    idx = np.clip(A_BAND - m, -MAX_REL, MAX_REL) + MAX_REL
    brow = jnp.take(a_rel_bias[0], jnp.asarray(idx, jnp.int32), axis=1)

    half = ROT_DIM // 2
    inv_freq = jnp.power(jnp.float32(ROPE_THETA), -jnp.arange(half, dtype=jnp.float32) * 2.0 / ROT_DIM)
    head_freq = jnp.concatenate([inv_freq, inv_freq, jnp.zeros((HEAD_DIM - ROT_DIM,), jnp.float32)])
    freq = jnp.tile(head_freq, LANES // HEAD_DIM)[None, :]

    lam_rows = jnp.zeros((8, LANES), jnp.float32)
    lam_rows = lam_rows.at[0:4, :HEAD_DIM].set(
        jnp.stack([b_lambda_q1[0], b_lambda_k1[0], b_lambda_q2[0], b_lambda_k2[0]]).astype(jnp.float32))

    qkvg = _proj_a(x, a_norm_pre[0][None, :], a_w_in[0].astype(bf16))
    y_a = _attn_a(qkvg, brow)
    h1, kvqg = _proj_b(y_a, x, a_w_out[0].astype(bf16), a_norm_post[0][None, :], positions[..., None], freq,
                       kv_norm[None, :], b_norm_pre[0][None, :], kv_w.astype(bf16), b_w_in[0].astype(bf16))
    y_b = _attn_b(kvqg, lam_rows, b_subln[0][None, :])
    return _out_proj(y_b, h1, b_w_out[0].astype(bf16), b_norm_post[0][None, :], "out_b")
```

```python
import math

import jax
import jax.numpy as jnp
import numpy as np
from jax import lax
from jax.experimental import pallas as pl
from jax.experimental.pallas import tpu as pltpu

D_MODEL = 1024
CHUNK = 64
PAST_CHUNKS = 8
HEAD_DIM = 64
A_HEADS = 16
B_HEADS = 8
MAX_REL = 128
ROT_DIM = 16
ROPE_THETA = 500000.0
RMS_EPS = 1e-6
NEG_INF = -1e30
LAM_INIT_B = 0.8 - 0.6 * math.exp(-0.3 * 1)
LOG2E = math.log2(math.e)
Q_SCALE = LOG2E / math.sqrt(HEAD_DIM)

LANES = 128
N_GROUPS = D_MODEL // LANES
TM = 512
TM_WIDE = 1024
TQ = 256
A_BAND = PAST_CHUNKS * CHUNK + TQ
VMEM_LIMIT = 56 * 1024 * 1024

_NT = (((1,), (1,)), ((), ()))
_ARB2 = pltpu.CompilerParams(dimension_semantics=("arbitrary", "arbitrary"),
                             vmem_limit_bytes=VMEM_LIMIT)


def _rms_scale(x):
    return lax.rsqrt(jnp.mean(x * x, axis=-1, keepdims=True) + RMS_EPS)


def _sigmoid(x):
    return 1.0 / (1.0 + jnp.exp(-x))


def _store_groups(o_ref, base, r):
    for gi in range(N_GROUPS):
        o_ref[base + gi] = r[:, gi * LANES:(gi + 1) * LANES].astype(o_ref.dtype)


def _group_spec(S, slab):
    return pl.BlockSpec((None, None, S, LANES), lambda b, h: (b, slab * N_GROUPS + h, 0, 0))


def _proj_a_kernel(x_ref, g_ref, w_ref, o_ref):
    x = x_ref[...]
    y = (x * _rms_scale(x) * g_ref[...]).astype(jnp.bfloat16)
    for c in range(4):
        r = jnp.dot(y, w_ref[:, c * D_MODEL:(c + 1) * D_MODEL], preferred_element_type=jnp.float32)
        if c == 0:
            r = r * Q_SCALE
        _store_groups(o_ref, c * N_GROUPS, r)


def _proj_a(x, gain, w):
    B, S, _ = x.shape
    return pl.pallas_call(
        _proj_a_kernel,
        grid=(B, S // TM_WIDE),
        in_specs=[
            pl.BlockSpec((None, TM_WIDE, D_MODEL), lambda b, i: (b, i, 0)),
            pl.BlockSpec((1, D_MODEL), lambda b, i: (0, 0)),
            pl.BlockSpec((D_MODEL, 4 * D_MODEL), lambda b, i: (0, 0)),
        ],
        out_specs=pl.BlockSpec((None, 4 * N_GROUPS, TM_WIDE, LANES), lambda b, i: (b, 0, i, 0)),
        out_shape=jax.ShapeDtypeStruct((B, 4 * N_GROUPS, S, LANES), jnp.bfloat16),
        compiler_params=_ARB2,
        name="proj_a",
    )(x, gain, w)


def _attn_a_kernel(q_ref, k_ref, v_ref, g_ref, brow_ref, y_ref, bias_scr, vx_scr, s_scr, p_scr):
    b = pl.program_id(0)
    hp = pl.program_id(1)
    S = q_ref.shape[0]

    @pl.when((b == 0) & (hp == 0))
    def _build_bias():
        r_chunk = lax.broadcasted_iota(jnp.int32, (TQ, A_BAND), 0) // CHUNK
        j_chunk = lax.broadcasted_iota(jnp.int32, (TQ, A_BAND), 1) // CHUNK
        in_band = (j_chunk >= r_chunk) & (j_chunk <= r_chunk + PAST_CHUNKS)
        for h in range(A_HEADS):
            row = jnp.broadcast_to(brow_ref[h:h + 1, :], (TQ, brow_ref.shape[1]))
            toep = pltpu.roll(row, 0, 1, stride=1, stride_axis=0)
            bias_scr[h] = jnp.where(in_band, toep[:, TQ:] * LOG2E, NEG_INF)

    vx_scr[:, :LANES] = v_ref[...]
    vx_scr[:, LANES:] = jnp.ones((S, LANES), vx_scr.dtype)

    low = lax.broadcasted_iota(jnp.int32, (TQ, LANES), 1) < HEAD_DIM

    def keys_of(qi):
        return slice(max(0, (qi + 1) * TQ - A_BAND), (qi + 1) * TQ)

    def scores(n):
        qi, half = divmod(n, 2)
        keys = keys_of(qi)
        nk = keys.stop - keys.start
        q2 = q_ref[qi * TQ:(qi + 1) * TQ, :]
        qm = jnp.where(low if half == 0 else ~low, q2, jnp.zeros_like(q2))
        s = lax.dot_general(qm, k_ref[keys, :], _NT, preferred_element_type=jnp.float32)
        s_scr[n, :, :nk] = s + bias_scr[2 * hp + half, :, A_BAND - nk:]

    def softmax(n):
        keys = keys_of(n // 2)
        nk = keys.stop - keys.start
        s = s_scr[n, :, :nk]
        m = jnp.max(s, axis=1, keepdims=True)
        p_scr[n, :, :nk] = jnp.exp2(s - m).astype(p_scr.dtype)

    def values(n):
        keys = keys_of(n // 2)
        nk = keys.stop - keys.start
        ol = jnp.dot(p_scr[n, :, :nk], vx_scr[keys, :], preferred_element_type=jnp.float32)
        return ol[:, :LANES] / ol[:, LANES:]

    n_stages = 2 * (S // TQ)
    order = list(range(n_stages - 1, -1, -1))
    outs = {}

    def finish(n):
        outs[n] = values(n)
        if n % 2 == 0:
            rows = slice((n // 2) * TQ, (n // 2 + 1) * TQ)
            o_pair = jnp.where(low, outs.pop(n), outs.pop(n + 1))
            gate = g_ref[rows, :].astype(jnp.float32)
            y_ref[rows, :] = (o_pair * (gate * _sigmoid(gate))).astype(y_ref.dtype)

    scores(order[0])
    for i, n in enumerate(order):
        if i + 1 < n_stages:
            scores(order[i + 1])
        softmax(n)
        if i > 0:
            finish(order[i - 1])
    finish(order[-1])


def _attn_a(qkvg, brow):
    B, _, S, _ = qkvg.shape
    return pl.pallas_call(
        _attn_a_kernel,
        grid=(B, N_GROUPS),
        in_specs=[_group_spec(S, 0), _group_spec(S, 1), _group_spec(S, 2), _group_spec(S, 3),
                  pl.BlockSpec(brow.shape, lambda b, h: (0, 0))],
        out_specs=pl.BlockSpec((None, None, S, LANES), lambda b, h: (b, h, 0, 0)),
        out_shape=jax.ShapeDtypeStruct((B, N_GROUPS, S, LANES), jnp.bfloat16),
        scratch_shapes=[
            pltpu.VMEM((A_HEADS, TQ, A_BAND), jnp.float32),
            pltpu.VMEM((S, 2 * LANES), jnp.bfloat16),
            pltpu.VMEM((2 * (S // TQ), TQ, A_BAND), jnp.float32),
            pltpu.VMEM((2 * (S // TQ), TQ, A_BAND), jnp.bfloat16),
        ],
        compiler_params=_ARB2,
        name="attn_a",
    )(qkvg, qkvg, qkvg, qkvg, brow)


def _out_kernel(y_ref, x_ref, w_ref, g_ref, o_ref):
    y = jnp.concatenate([y_ref[gi] for gi in range(N_GROUPS)], axis=1)
    yo = jnp.dot(y, w_ref[...], preferred_element_type=jnp.float32)
    o_ref[...] = x_ref[...] + yo * _rms_scale(yo) * g_ref[...]


def _out_proj(y, x, w, gain, name):
    B, S, _ = x.shape
    return pl.pallas_call(
        _out_kernel,
        grid=(B, S // TM_WIDE),
        in_specs=[
            pl.BlockSpec((None, N_GROUPS, TM_WIDE, LANES), lambda b, i: (b, 0, i, 0)),
            pl.BlockSpec((None, TM_WIDE, D_MODEL), lambda b, i: (b, i, 0)),
            pl.BlockSpec((D_MODEL, D_MODEL), lambda b, i: (0, 0)),
            pl.BlockSpec((1, D_MODEL), lambda b, i: (0, 0)),
        ],
        out_specs=pl.BlockSpec((None, TM_WIDE, D_MODEL), lambda b, i: (b, i, 0)),
        out_shape=jax.ShapeDtypeStruct((B, S, D_MODEL), jnp.float32),
        compiler_params=_ARB2,
        name=name,
    )(y, x, w, gain)


def _rope(t, cos, sin_lo, sin_hi):
    return t * cos + pltpu.roll(t, LANES - ROT_DIM // 2, 1) * sin_lo + pltpu.roll(t, ROT_DIM // 2, 1) * sin_hi


def _proj_b_kernel(y_ref, xin_ref, wout_ref, gpost_ref, pos_ref, freq_ref, cosr_ref, sinr_ref, gkv_ref, gb_ref,
                   wkv_ref, wb_ref, h_ref, o_ref):
    half = TM // 2
    halves = (slice(0, half), slice(half, TM))

    ang0 = pos_ref[0:1, :].astype(jnp.float32) * freq_ref[...]
    cos0 = jnp.cos(ang0)
    sin0 = jnp.sin(ang0)
    lane = lax.broadcasted_iota(jnp.int32, (half, LANES), 1) % HEAD_DIM

    def outproj(rows):
        y = jnp.concatenate([y_ref[gi, rows, :] for gi in range(N_GROUPS)], axis=1)
        return jnp.dot(y, wout_ref[...], preferred_element_type=jnp.float32)

    def norms(rows, yo):
        x = xin_ref[rows, :] + yo * _rms_scale(yo) * gpost_ref[...]
        h_ref[rows, :] = x
        xn = x * _rms_scale(x)
        return (xn * gkv_ref[...]).astype(jnp.bfloat16), (xn * gb_ref[...]).astype(jnp.bfloat16)

    def project(rows, u_kv, u_b):
        cos = cos0 * cosr_ref[rows, :] - sin0 * sinr_ref[rows, :]
        sin = sin0 * cosr_ref[rows, :] + cos0 * sinr_ref[rows, :]
        sin_lo = jnp.where(lane < ROT_DIM // 2, -sin, 0.0)
        sin_hi = jnp.where((lane >= ROT_DIM // 2) & (lane < ROT_DIM), sin, 0.0)

        def store(base, r, rope, scale=1.0):
            for gi in range(N_GROUPS):
                t = r[:, gi * LANES:(gi + 1) * LANES]
                if rope:
                    t = _rope(t, cos, sin_lo, sin_hi)
                if scale != 1.0:
                    t = t * scale
                o_ref[base + gi, rows, :] = t.astype(o_ref.dtype)

        store(0, jnp.dot(u_kv, wkv_ref[:, :D_MODEL], preferred_element_type=jnp.float32), True)
        store(N_GROUPS, jnp.dot(u_kv, wkv_ref[:, D_MODEL:], preferred_element_type=jnp.float32), False)
        store(2 * N_GROUPS, jnp.dot(u_b, wb_ref[:, :D_MODEL], preferred_element_type=jnp.float32), True, Q_SCALE)
        store(3 * N_GROUPS, jnp.dot(u_b, wb_ref[:, D_MODEL:], preferred_element_type=jnp.float32), False)

    yo0 = outproj(halves[0])
    yo1 = outproj(halves[1])
    u0 = norms(halves[0], yo0)
    project(halves[0], *u0)
    u1 = norms(halves[1], yo1)
    project(halves[1], *u1)


def _proj_b(y, x, wout, gpost, pos, freq, gkv, gb, wkv, wb):
    B, S, _ = x.shape
    row_ang = jnp.arange(TM, dtype=jnp.float32)[:, None] * freq
    cosr, sinr = jnp.cos(row_ang), jnp.sin(row_ang)
    return pl.pallas_call(
        _proj_b_kernel,
        grid=(B, S // TM),
        in_specs=[
            pl.BlockSpec((None, N_GROUPS, TM, LANES), lambda b, i: (b, 0, i, 0)),
            pl.BlockSpec((None, TM, D_MODEL), lambda b, i: (b, i, 0)),
            pl.BlockSpec((D_MODEL, D_MODEL), lambda b, i: (0, 0)),
            pl.BlockSpec((1, D_MODEL), lambda b, i: (0, 0)),
            pl.BlockSpec((None, TM, 1), lambda b, i: (b, i, 0)),
            pl.BlockSpec((1, LANES), lambda b, i: (0, 0)),
            pl.BlockSpec((TM, LANES), lambda b, i: (0, 0)),
            pl.BlockSpec((TM, LANES), lambda b, i: (0, 0)),
            pl.BlockSpec((1, D_MODEL), lambda b, i: (0, 0)),
            pl.BlockSpec((1, D_MODEL), lambda b, i: (0, 0)),
            pl.BlockSpec((D_MODEL, 2 * D_MODEL), lambda b, i: (0, 0)),
            pl.BlockSpec((D_MODEL, 2 * D_MODEL), lambda b, i: (0, 0)),
        ],
        out_specs=[
            pl.BlockSpec((None, TM, D_MODEL), lambda b, i: (b, i, 0)),
            pl.BlockSpec((None, 4 * N_GROUPS, TM, LANES), lambda b, i: (b, 0, i, 0)),
        ],
        out_shape=[
            jax.ShapeDtypeStruct((B, S, D_MODEL), jnp.float32),
            jax.ShapeDtypeStruct((B, 4 * N_GROUPS, S, LANES), jnp.bfloat16),
        ],
        compiler_params=_ARB2,
        name="proj_b",
    )(y, x, wout, gpost, pos, freq, cosr, sinr, gkv, gb, wkv, wb)


def _attn_b_kernel(k_ref, v_ref, q_ref, g_ref, lam_ref, subln_ref, y_ref, vx_scr, *stage_scr):
    S = q_ref.shape[0]
    n_q = S // TQ
    s_scr, p_scr = stage_scr[:n_q], stage_scr[n_q:]
    lp = lam_ref[...]
    lam = (jnp.exp(jnp.sum(lp[0:1] * lp[1:2], axis=1, keepdims=True))
           - jnp.exp(jnp.sum(lp[2:3] * lp[3:4], axis=1, keepdims=True)) + LAM_INIT_B)

    vx_scr[:, :LANES] = v_ref[...]
    vx_scr[:, LANES:] = jnp.ones((S, LANES), vx_scr.dtype)

    low = lax.broadcasted_iota(jnp.int32, (TQ, LANES), 1) < HEAD_DIM
    q_chunk = lax.broadcasted_iota(jnp.int32, (2 * TQ, TQ), 0) % TQ // CHUNK
    k_chunk = lax.broadcasted_iota(jnp.int32, (2 * TQ, TQ), 1) // CHUNK
    diag_mask = k_chunk <= q_chunk

    def scores(qi):
        q2 = q_ref[qi * TQ:(qi + 1) * TQ, :]
        zero = jnp.zeros_like(q2)
        qs = jnp.concatenate([jnp.where(low, q2, zero), jnp.where(low, zero, q2)], axis=0)
        if qi > 0:
            s_scr[qi][:, :qi * TQ] = lax.dot_general(qs, k_ref[:qi * TQ, :], _NT,
                                                     preferred_element_type=jnp.float32)
        s_d = lax.dot_general(qs, k_ref[qi * TQ:(qi + 1) * TQ, :], _NT, preferred_element_type=jnp.float32)
        s_scr[qi][:, qi * TQ:] = jnp.where(diag_mask, s_d, NEG_INF)

    def softmax(qi):
        s = s_scr[qi][...]
        m = jnp.max(s, axis=1, keepdims=True)
        p_scr[qi][...] = jnp.exp2(s - m).astype(p_scr[qi].dtype)

    def values(qi):
        nk = (qi + 1) * TQ
        rows = slice(qi * TQ, (qi + 1) * TQ)
        ol = jnp.dot(p_scr[qi][...], vx_scr[:nk, :], preferred_element_type=jnp.float32)
        o = ol[:TQ, :LANES] / ol[:TQ, LANES:] - lam * (ol[TQ:, :LANES] / ol[TQ:, LANES:])
        o = o * _rms_scale(o) * subln_ref[...] * (1.0 - LAM_INIT_B)
        gate = g_ref[rows, :].astype(jnp.float32)
        y_ref[rows, :] = (o * (gate * _sigmoid(gate))).astype(y_ref.dtype)

    order = list(range(n_q - 1, -1, -1))
    scores(order[0])
    for n, qi in enumerate(order):
        if n + 1 < n_q:
            scores(order[n + 1])
        softmax(qi)
        if n > 0:
            values(order[n - 1])
    values(order[-1])


def _attn_b(kvqg, lam_rows, subln):
    B, _, S, _ = kvqg.shape
    n_q = S // TQ
    return pl.pallas_call(
        _attn_b_kernel,
        grid=(B, B_HEADS),
        in_specs=[_group_spec(S, 0), _group_spec(S, 1), _group_spec(S, 2), _group_spec(S, 3),
                  pl.BlockSpec(lam_rows.shape, lambda b, h: (0, 0)),
                  pl.BlockSpec((1, LANES), lambda b, h: (0, 0))],
        out_specs=pl.BlockSpec((None, None, S, LANES), lambda b, h: (b, h, 0, 0)),
        out_shape=jax.ShapeDtypeStruct((B, B_HEADS, S, LANES), jnp.bfloat16),
        scratch_shapes=(
            [pltpu.VMEM((S, 2 * LANES), jnp.bfloat16)]
            + [pltpu.VMEM((2 * TQ, (qi + 1) * TQ), jnp.float32) for qi in range(n_q)]
            + [pltpu.VMEM((2 * TQ, (qi + 1) * TQ), jnp.bfloat16) for qi in range(n_q)]),
        compiler_params=_ARB2,
        name="attn_b",
    )(kvqg, kvqg, kvqg, kvqg, lam_rows, subln)


def kernel(x, positions, a_norm_pre, a_w_in, a_rel_bias, a_w_out, a_norm_post, kv_norm, kv_w,
           b_norm_pre, b_w_in, b_lambda_q1, b_lambda_k1, b_lambda_q2, b_lambda_k2, b_subln, b_w_out,
           b_norm_post):
    assert a_w_in.shape[0] == 1 and b_w_in.shape[0] == 1
    bf16 = jnp.bfloat16

    m = np.arange(A_BAND + TQ)
    idx = np.clip(A_BAND - m, -MAX_REL, MAX_REL) + MAX_REL
    brow = jnp.take(a_rel_bias[0], jnp.asarray(idx, jnp.int32), axis=1)

    half = ROT_DIM // 2
    inv_freq = jnp.power(jnp.float32(ROPE_THETA), -jnp.arange(half, dtype=jnp.float32) * 2.0 / ROT_DIM)
    head_freq = jnp.concatenate([inv_freq, inv_freq, jnp.zeros((HEAD_DIM - ROT_DIM,), jnp.float32)])
    freq = jnp.tile(head_freq, LANES // HEAD_DIM)[None, :]

    lam_rows = jnp.zeros((8, LANES), jnp.float32)
    lam_rows = lam_rows.at[0:4, :HEAD_DIM].set(
        jnp.stack([b_lambda_q1[0], b_lambda_k1[0], b_lambda_q2[0], b_lambda_k2[0]]).astype(jnp.float32))

    qkvg = _proj_a(x, a_norm_pre[0][None, :], a_w_in[0].astype(bf16))
    y_a = _attn_a(qkvg, brow)
    h1, kvqg = _proj_b(y_a, x, a_w_out[0].astype(bf16), a_norm_post[0][None, :], positions[..., None], freq,
                       kv_norm[None, :], b_norm_pre[0][None, :], kv_w.astype(bf16), b_w_in[0].astype(bf16))
    y_b = _attn_b(kvqg, lam_rows, b_subln[0][None, :])
    return _out_proj(y_b, h1, b_w_out[0].astype(bf16), b_norm_post[0][None, :], "out_b")
```

```python
import math

import jax
import jax.numpy as jnp
import numpy as np
from jax import lax
from jax.experimental import pallas as pl
from jax.experimental.pallas import tpu as pltpu

D_MODEL = 1024
CHUNK = 64
PAST_CHUNKS = 8
HEAD_DIM = 64
A_HEADS = 16
B_HEADS = 8
MAX_REL = 128
ROT_DIM = 16
ROPE_THETA = 500000.0
RMS_EPS = 1e-6
NEG_INF = -1e30
LAM_INIT_B = 0.8 - 0.6 * math.exp(-0.3 * 1)
LOG2E = math.log2(math.e)
Q_SCALE = LOG2E / math.sqrt(HEAD_DIM)

LANES = 128
N_GROUPS = D_MODEL // LANES
TM = 512
TM_WIDE = 1024
TQ = 256
A_BAND = PAST_CHUNKS * CHUNK + TQ
VMEM_LIMIT = 56 * 1024 * 1024

_NT = (((1,), (1,)), ((), ()))
_ARB2 = pltpu.CompilerParams(dimension_semantics=("arbitrary", "arbitrary"),
                             vmem_limit_bytes=VMEM_LIMIT)


def _rms_scale(x):
    return lax.rsqrt(jnp.mean(x * x, axis=-1, keepdims=True) + RMS_EPS)


def _sigmoid(x):
    return 1.0 / (1.0 + jnp.exp(-x))


def _store_groups(o_ref, base, r):
    for gi in range(N_GROUPS):
        o_ref[base + gi] = r[:, gi * LANES:(gi + 1) * LANES].astype(o_ref.dtype)


def _group_spec(S, slab):
    return pl.BlockSpec((None, None, S, LANES), lambda b, h: (b, slab * N_GROUPS + h, 0, 0))


def _proj_a_kernel(x_ref, g_ref, w_ref, o_ref):
    x = x_ref[...]
    y = (x * _rms_scale(x) * g_ref[...]).astype(jnp.bfloat16)
    for c in range(4):
        r = jnp.dot(y, w_ref[:, c * D_MODEL:(c + 1) * D_MODEL], preferred_element_type=jnp.float32)
        if c == 0:
            r = r * Q_SCALE
        _store_groups(o_ref, c * N_GROUPS, r)


def _proj_a(x, gain, w):
    B, S, _ = x.shape
    return pl.pallas_call(
        _proj_a_kernel,
        grid=(B, S // TM_WIDE),
        in_specs=[
            pl.BlockSpec((None, TM_WIDE, D_MODEL), lambda b, i: (b, i, 0)),
            pl.BlockSpec((1, D_MODEL), lambda b, i: (0, 0)),
            pl.BlockSpec((D_MODEL, 4 * D_MODEL), lambda b, i: (0, 0)),
        ],
        out_specs=pl.BlockSpec((None, 4 * N_GROUPS, TM_WIDE, LANES), lambda b, i: (b, 0, i, 0)),
        out_shape=jax.ShapeDtypeStruct((B, 4 * N_GROUPS, S, LANES), jnp.bfloat16),
        compiler_params=_ARB2,
        name="proj_a",
    )(x, gain, w)


def _attn_a_kernel(q_ref, k_ref, v_ref, g_ref, brow_ref, y_ref, bias_scr, vx_scr, s_scr, p_scr):
    b = pl.program_id(0)
    hp = pl.program_id(1)
    S = q_ref.shape[0]

    @pl.when((b == 0) & (hp == 0))
    def _build_bias():
        r_chunk = lax.broadcasted_iota(jnp.int32, (TQ, A_BAND), 0) // CHUNK
        j_chunk = lax.broadcasted_iota(jnp.int32, (TQ, A_BAND), 1) // CHUNK
        in_band = (j_chunk >= r_chunk) & (j_chunk <= r_chunk + PAST_CHUNKS)
        for h in range(A_HEADS):
            row = jnp.broadcast_to(brow_ref[h:h + 1, :], (TQ, brow_ref.shape[1]))
            toep = pltpu.roll(row, 0, 1, stride=1, stride_axis=0)
            bias_scr[h] = jnp.where(in_band, toep[:, TQ:] * LOG2E, NEG_INF)

    vx_scr[:, :LANES] = v_ref[...]
    vx_scr[:, LANES:] = jnp.ones((S, LANES), vx_scr.dtype)

    low = lax.broadcasted_iota(jnp.int32, (TQ, LANES), 1) < HEAD_DIM

    def keys_of(qi):
        return slice(max(0, (qi + 1) * TQ - A_BAND), (qi + 1) * TQ)

    def scores(qi):
        keys = keys_of(qi)
        nk = keys.stop - keys.start
        q2 = q_ref[qi * TQ:(qi + 1) * TQ, :]
        zero = jnp.zeros_like(q2)
        qs = jnp.concatenate([jnp.where(low, q2, zero), jnp.where(low, zero, q2)], axis=0)
        s = lax.dot_general(qs, k_ref[keys, :], _NT, preferred_element_type=jnp.float32)
        s_scr[qi, :TQ, :nk] = s[:TQ] + bias_scr[2 * hp, :, A_BAND - nk:]
        s_scr[qi, TQ:, :nk] = s[TQ:] + bias_scr[2 * hp + 1, :, A_BAND - nk:]

    def softmax(qi):
        nk = keys_of(qi).stop - keys_of(qi).start
        s = s_scr[qi, :, :nk]
        m = jnp.max(s, axis=1, keepdims=True)
        p_scr[qi, :, :nk] = jnp.exp2(s - m).astype(p_scr.dtype)

    def values(qi):
        keys = keys_of(qi)
        nk = keys.stop - keys.start
        rows = slice(qi * TQ, (qi + 1) * TQ)
        ol = jnp.dot(p_scr[qi, :, :nk], vx_scr[keys, :], preferred_element_type=jnp.float32)
        o = ol[:, :LANES] / ol[:, LANES:]
        o_pair = jnp.where(low, o[:TQ], o[TQ:])
        gate = g_ref[rows, :].astype(jnp.float32)
        y_ref[rows, :] = (o_pair * (gate * _sigmoid(gate))).astype(y_ref.dtype)

    n_q = S // TQ
    order = list(range(n_q - 1, -1, -1))
    scores(order[0])
    for n, qi in enumerate(order):
        if n + 1 < n_q:
            scores(order[n + 1])
        softmax(qi)
        if n > 0:
            values(order[n - 1])
    values(order[-1])


def _attn_a(qkvg, brow):
    B, _, S, _ = qkvg.shape
    return pl.pallas_call(
        _attn_a_kernel,
        grid=(B, N_GROUPS),
        in_specs=[_group_spec(S, 0), _group_spec(S, 1), _group_spec(S, 2), _group_spec(S, 3),
                  pl.BlockSpec(brow.shape, lambda b, h: (0, 0))],
        out_specs=pl.BlockSpec((None, None, S, LANES), lambda b, h: (b, h, 0, 0)),
        out_shape=jax.ShapeDtypeStruct((B, N_GROUPS, S, LANES), jnp.bfloat16),
        scratch_shapes=[
            pltpu.VMEM((A_HEADS, TQ, A_BAND), jnp.float32),
            pltpu.VMEM((S, 2 * LANES), jnp.bfloat16),
            pltpu.VMEM((S // TQ, 2 * TQ, A_BAND), jnp.float32),
            pltpu.VMEM((S // TQ, 2 * TQ, A_BAND), jnp.bfloat16),
        ],
        compiler_params=_ARB2,
        name="attn_a",
    )(qkvg, qkvg, qkvg, qkvg, brow)


def _out_kernel(y_ref, x_ref, w_ref, g_ref, o_ref):
    y = jnp.concatenate([y_ref[gi] for gi in range(N_GROUPS)], axis=1)
    yo = jnp.dot(y, w_ref[...], preferred_element_type=jnp.float32)
    o_ref[...] = x_ref[...] + yo * _rms_scale(yo) * g_ref[...]


def _out_proj(y, x, w, gain, name):
    B, S, _ = x.shape
    return pl.pallas_call(
        _out_kernel,
        grid=(B, S // TM_WIDE),
        in_specs=[
            pl.BlockSpec((None, N_GROUPS, TM_WIDE, LANES), lambda b, i: (b, 0, i, 0)),
            pl.BlockSpec((None, TM_WIDE, D_MODEL), lambda b, i: (b, i, 0)),
            pl.BlockSpec((D_MODEL, D_MODEL), lambda b, i: (0, 0)),
            pl.BlockSpec((1, D_MODEL), lambda b, i: (0, 0)),
        ],
        out_specs=pl.BlockSpec((None, TM_WIDE, D_MODEL), lambda b, i: (b, i, 0)),
        out_shape=jax.ShapeDtypeStruct((B, S, D_MODEL), jnp.float32),
        compiler_params=_ARB2,
        name=name,
    )(y, x, w, gain)


def _rope(t, cos, sin_lo, sin_hi):
    return t * cos + pltpu.roll(t, LANES - ROT_DIM // 2, 1) * sin_lo + pltpu.roll(t, ROT_DIM // 2, 1) * sin_hi


def _proj_b_kernel(y_ref, xin_ref, wout_ref, gpost_ref, pos_ref, freq_ref, cosr_ref, sinr_ref, gkv_ref, gb_ref,
                   wkv_ref, wb_ref, h_ref, o_ref):
    half = TM // 2
    halves = (slice(0, half), slice(half, TM))

    ang0 = pos_ref[0:1, :].astype(jnp.float32) * freq_ref[...]
    cos0 = jnp.cos(ang0)
    sin0 = jnp.sin(ang0)
    lane = lax.broadcasted_iota(jnp.int32, (half, LANES), 1) % HEAD_DIM

    def outproj(rows):
        y = jnp.concatenate([y_ref[gi, rows, :] for gi in range(N_GROUPS)], axis=1)
        return jnp.dot(y, wout_ref[...], preferred_element_type=jnp.float32)

    def norms(rows, yo):
        x = xin_ref[rows, :] + yo * _rms_scale(yo) * gpost_ref[...]
        h_ref[rows, :] = x
        xn = x * _rms_scale(x)
        return (xn * gkv_ref[...]).astype(jnp.bfloat16), (xn * gb_ref[...]).astype(jnp.bfloat16)

    def project(rows, u_kv, u_b):
        cos = cos0 * cosr_ref[rows, :] - sin0 * sinr_ref[rows, :]
        sin = sin0 * cosr_ref[rows, :] + cos0 * sinr_ref[rows, :]
        sin_lo = jnp.where(lane < ROT_DIM // 2, -sin, 0.0)
        sin_hi = jnp.where((lane >= ROT_DIM // 2) & (lane < ROT_DIM), sin, 0.0)

        def store(base, r, rope, scale=1.0):
            for gi in range(N_GROUPS):
                t = r[:, gi * LANES:(gi + 1) * LANES]
                if rope:
                    t = _rope(t, cos, sin_lo, sin_hi)
                if scale != 1.0:
                    t = t * scale
                o_ref[base + gi, rows, :] = t.astype(o_ref.dtype)

        store(0, jnp.dot(u_kv, wkv_ref[:, :D_MODEL], preferred_element_type=jnp.float32), True)
        store(N_GROUPS, jnp.dot(u_kv, wkv_ref[:, D_MODEL:], preferred_element_type=jnp.float32), False)
        store(2 * N_GROUPS, jnp.dot(u_b, wb_ref[:, :D_MODEL], preferred_element_type=jnp.float32), True, Q_SCALE)
        store(3 * N_GROUPS, jnp.dot(u_b, wb_ref[:, D_MODEL:], preferred_element_type=jnp.float32), False)

    yo0 = outproj(halves[0])
    yo1 = outproj(halves[1])
    u0 = norms(halves[0], yo0)
    project(halves[0], *u0)
    u1 = norms(halves[1], yo1)
    project(halves[1], *u1)


def _proj_b(y, x, wout, gpost, pos, freq, gkv, gb, wkv, wb):
    B, S, _ = x.shape
    row_ang = jnp.arange(TM, dtype=jnp.float32)[:, None] * freq
    cosr, sinr = jnp.cos(row_ang), jnp.sin(row_ang)
    return pl.pallas_call(
        _proj_b_kernel,
        grid=(B, S // TM),
        in_specs=[
            pl.BlockSpec((None, N_GROUPS, TM, LANES), lambda b, i: (b, 0, i, 0)),
            pl.BlockSpec((None, TM, D_MODEL), lambda b, i: (b, i, 0)),
            pl.BlockSpec((D_MODEL, D_MODEL), lambda b, i: (0, 0)),
            pl.BlockSpec((1, D_MODEL), lambda b, i: (0, 0)),
            pl.BlockSpec((None, TM, 1), lambda b, i: (b, i, 0)),
            pl.BlockSpec((1, LANES), lambda b, i: (0, 0)),
            pl.BlockSpec((TM, LANES), lambda b, i: (0, 0)),
            pl.BlockSpec((TM, LANES), lambda b, i: (0, 0)),
            pl.BlockSpec((1, D_MODEL), lambda b, i: (0, 0)),
            pl.BlockSpec((1, D_MODEL), lambda b, i: (0, 0)),
            pl.BlockSpec((D_MODEL, 2 * D_MODEL), lambda b, i: (0, 0)),
            pl.BlockSpec((D_MODEL, 2 * D_MODEL), lambda b, i: (0, 0)),
        ],
        out_specs=[
            pl.BlockSpec((None, TM, D_MODEL), lambda b, i: (b, i, 0)),
            pl.BlockSpec((None, 4 * N_GROUPS, TM, LANES), lambda b, i: (b, 0, i, 0)),
        ],
        out_shape=[
            jax.ShapeDtypeStruct((B, S, D_MODEL), jnp.float32),
            jax.ShapeDtypeStruct((B, 4 * N_GROUPS, S, LANES), jnp.bfloat16),
        ],
        compiler_params=_ARB2,
        name="proj_b",
    )(y, x, wout, gpost, pos, freq, cosr, sinr, gkv, gb, wkv, wb)


def _attn_b_kernel(k_ref, v_ref, q_ref, g_ref, lam_ref, subln_ref, y_ref, vx_scr, *stage_scr):
    S = q_ref.shape[0]
    n_q = S // TQ
    s_scr, p_scr = stage_scr[:n_q], stage_scr[n_q:]
    lp = lam_ref[...]
    lam = (jnp.exp(jnp.sum(lp[0:1] * lp[1:2], axis=1, keepdims=True))
           - jnp.exp(jnp.sum(lp[2:3] * lp[3:4], axis=1, keepdims=True)) + LAM_INIT_B)

    vx_scr[:, :LANES] = v_ref[...]
    vx_scr[:, LANES:] = jnp.ones((S, LANES), vx_scr.dtype)

    low = lax.broadcasted_iota(jnp.int32, (TQ, LANES), 1) < HEAD_DIM
    q_chunk = lax.broadcasted_iota(jnp.int32, (2 * TQ, TQ), 0) % TQ // CHUNK
    k_chunk = lax.broadcasted_iota(jnp.int32, (2 * TQ, TQ), 1) // CHUNK
    diag_mask = k_chunk <= q_chunk

    def scores(qi):
        q2 = q_ref[qi * TQ:(qi + 1) * TQ, :]
        zero = jnp.zeros_like(q2)
        qs = jnp.concatenate([jnp.where(low, q2, zero), jnp.where(low, zero, q2)], axis=0)
        if qi > 0:
            s_scr[qi][:, :qi * TQ] = lax.dot_general(qs, k_ref[:qi * TQ, :], _NT,
                                                     preferred_element_type=jnp.float32)
        s_d = lax.dot_general(qs, k_ref[qi * TQ:(qi + 1) * TQ, :], _NT, preferred_element_type=jnp.float32)
        s_scr[qi][:, qi * TQ:] = jnp.where(diag_mask, s_d, NEG_INF)

    def softmax(qi):
        s = s_scr[qi][...]
        m = jnp.max(s, axis=1, keepdims=True)
        p_scr[qi][...] = jnp.exp2(s - m).astype(p_scr[qi].dtype)

    def values(qi):
        nk = (qi + 1) * TQ
        rows = slice(qi * TQ, (qi + 1) * TQ)
        ol = jnp.dot(p_scr[qi][...], vx_scr[:nk, :], preferred_element_type=jnp.float32)
        o = ol[:TQ, :LANES] / ol[:TQ, LANES:] - lam * (ol[TQ:, :LANES] / ol[TQ:, LANES:])
        o = o * _rms_scale(o) * subln_ref[...] * (1.0 - LAM_INIT_B)
        gate = g_ref[rows, :].astype(jnp.float32)
        y_ref[rows, :] = (o * (gate * _sigmoid(gate))).astype(y_ref.dtype)

    order = list(range(n_q - 1, -1, -1))
    scores(order[0])
    for n, qi in enumerate(order):
        if n + 1 < n_q:
            scores(order[n + 1])
        softmax(qi)
        if n > 0:
            values(order[n - 1])
    values(order[-1])


def _attn_b(kvqg, lam_rows, subln):
    B, _, S, _ = kvqg.shape
    n_q = S // TQ
    return pl.pallas_call(
        _attn_b_kernel,
        grid=(B, B_HEADS),
        in_specs=[_group_spec(S, 0), _group_spec(S, 1), _group_spec(S, 2), _group_spec(S, 3),
                  pl.BlockSpec(lam_rows.shape, lambda b, h: (0, 0)),
                  pl.BlockSpec((1, LANES), lambda b, h: (0, 0))],
        out_specs=pl.BlockSpec((None, None, S, LANES), lambda b, h: (b, h, 0, 0)),
        out_shape=jax.ShapeDtypeStruct((B, B_HEADS, S, LANES), jnp.bfloat16),
        scratch_shapes=(
            [pltpu.VMEM((S, 2 * LANES), jnp.bfloat16)]
            + [pltpu.VMEM((2 * TQ, (qi + 1) * TQ), jnp.float32) for qi in range(n_q)]
            + [pltpu.VMEM((2 * TQ, (qi + 1) * TQ), jnp.bfloat16) for qi in range(n_q)]),
        compiler_params=_ARB2,
        name="attn_b",
    )(kvqg, kvqg, kvqg, kvqg, lam_rows, subln)


def kernel(x, positions, a_norm_pre, a_w_in, a_rel_bias, a_w_out, a_norm_post, kv_norm, kv_w,
           b_norm_pre, b_w_in, b_lambda_q1, b_lambda_k1, b_lambda_q2, b_lambda_k2, b_subln, b_w_out,
           b_norm_post):
    assert a_w_in.shape[0] == 1 and b_w_in.shape[0] == 1
    bf16 = jnp.bfloat16

    m = np.arange(A_BAND + TQ)
    idx = np.clip(A_BAND - m, -MAX_REL, MAX_REL) + MAX_REL
    brow = jnp.take(a_rel_bias[0], jnp.asarray(idx, jnp.int32), axis=1)

    half = ROT_DIM // 2
    inv_freq = jnp.power(jnp.float32(ROPE_THETA), -jnp.arange(half, dtype=jnp.float32) * 2.0 / ROT_DIM)
    head_freq = jnp.concatenate([inv_freq, inv_freq, jnp.zeros((HEAD_DIM - ROT_DIM,), jnp.float32)])
    freq = jnp.tile(head_freq, LANES // HEAD_DIM)[None, :]

    lam_rows = jnp.zeros((8, LANES), jnp.float32)
    lam_rows = lam_rows.at[0:4, :HEAD_DIM].set(
        jnp.stack([b_lambda_q1[0], b_lambda_k1[0], b_lambda_q2[0], b_lambda_k2[0]]).astype(jnp.float32))

    qkvg = _proj_a(x, a_norm_pre[0][None, :], a_w_in[0].astype(bf16))
    y_a = _attn_a(qkvg, brow)
    h1, kvqg = _proj_b(y_a, x, a_w_out[0].astype(bf16), a_norm_post[0][None, :], positions[..., None], freq,
                       kv_norm[None, :], b_norm_pre[0][None, :], kv_w.astype(bf16), b_w_in[0].astype(bf16))
    y_b = _attn_b(kvqg, lam_rows, b_subln[0][None, :])
    return _out_proj(y_b, h1, b_w_out[0].astype(bf16), b_norm_post[0][None, :], "out_b")
```

```python
import math

import jax
import jax.numpy as jnp
import numpy as np
from jax import lax
from jax.experimental import pallas as pl
from jax.experimental.pallas import tpu as pltpu

D_MODEL = 1024
CHUNK = 64
PAST_CHUNKS = 8
HEAD_DIM = 64
A_HEADS = 16
B_HEADS = 8
MAX_REL = 128
ROT_DIM = 16
ROPE_THETA = 500000.0
RMS_EPS = 1e-6
NEG_INF = -1e30
LAM_INIT_B = 0.8 - 0.6 * math.exp(-0.3 * 1)
LOG2E = math.log2(math.e)
Q_SCALE = LOG2E / math.sqrt(HEAD_DIM)

LANES = 128
N_GROUPS = D_MODEL // LANES
TM = 512
TM_WIDE = 1024
TQ = 256
A_BAND = PAST_CHUNKS * CHUNK + TQ
VMEM_LIMIT = 56 * 1024 * 1024

_NT = (((1,), (1,)), ((), ()))
_ARB2 = pltpu.CompilerParams(dimension_semantics=("arbitrary", "arbitrary"),
                             vmem_limit_bytes=VMEM_LIMIT)


def _rms_scale(x):
    return lax.rsqrt(jnp.mean(x * x, axis=-1, keepdims=True) + RMS_EPS)


def _sigmoid(x):
    return 1.0 / (1.0 + jnp.exp(-x))


def _store_groups(o_ref, base, r):
    for gi in range(N_GROUPS):
        o_ref[base + gi] = r[:, gi * LANES:(gi + 1) * LANES].astype(o_ref.dtype)


def _group_spec(S, slab):
    return pl.BlockSpec((None, None, S, LANES), lambda b, h: (b, slab * N_GROUPS + h, 0, 0))


def _proj_a_kernel(x_ref, g_ref, w_ref, o_ref):
    x = x_ref[...]
    y = (x * _rms_scale(x) * g_ref[...]).astype(jnp.bfloat16)
    for c in range(4):
        r = jnp.dot(y, w_ref[:, c * D_MODEL:(c + 1) * D_MODEL], preferred_element_type=jnp.float32)
        if c == 0:
            r = r * Q_SCALE
        _store_groups(o_ref, c * N_GROUPS, r)


def _proj_a(x, gain, w):
    B, S, _ = x.shape
    return pl.pallas_call(
        _proj_a_kernel,
        grid=(B, S // TM_WIDE),
        in_specs=[
            pl.BlockSpec((None, TM_WIDE, D_MODEL), lambda b, i: (b, i, 0)),
            pl.BlockSpec((1, D_MODEL), lambda b, i: (0, 0)),
            pl.BlockSpec((D_MODEL, 4 * D_MODEL), lambda b, i: (0, 0)),
        ],
        out_specs=pl.BlockSpec((None, 4 * N_GROUPS, TM_WIDE, LANES), lambda b, i: (b, 0, i, 0)),
        out_shape=jax.ShapeDtypeStruct((B, 4 * N_GROUPS, S, LANES), jnp.bfloat16),
        compiler_params=_ARB2,
        name="proj_a",
    )(x, gain, w)


def _attn_a_kernel(q_ref, k_ref, v_ref, g_ref, brow_ref, y_ref, bias_scr, vx_scr, s_scr, p_scr):
    b = pl.program_id(0)
    hp = pl.program_id(1)
    S = q_ref.shape[0]

    @pl.when((b == 0) & (hp == 0))
    def _build_bias():
        r_chunk = lax.broadcasted_iota(jnp.int32, (TQ, A_BAND), 0) // CHUNK
        j_chunk = lax.broadcasted_iota(jnp.int32, (TQ, A_BAND), 1) // CHUNK
        in_band = (j_chunk >= r_chunk) & (j_chunk <= r_chunk + PAST_CHUNKS)
        for h in range(A_HEADS):
            row = jnp.broadcast_to(brow_ref[h:h + 1, :], (TQ, brow_ref.shape[1]))
            toep = pltpu.roll(row, 0, 1, stride=1, stride_axis=0)
            bias_scr[h] = jnp.where(in_band, toep[:, TQ:] * LOG2E, NEG_INF)

    vx_scr[:, :LANES] = v_ref[...]
    vx_scr[:, LANES:] = jnp.ones((S, LANES), vx_scr.dtype)

    low = lax.broadcasted_iota(jnp.int32, (TQ, LANES), 1) < HEAD_DIM

    def keys_of(qi):
        return slice(max(0, (qi + 1) * TQ - A_BAND), (qi + 1) * TQ)

    def scores(n):
        qi, half = divmod(n, 2)
        keys = keys_of(qi)
        nk = keys.stop - keys.start
        q2 = q_ref[qi * TQ:(qi + 1) * TQ, :]
        qm = jnp.where(low if half == 0 else ~low, q2, jnp.zeros_like(q2))
        s = lax.dot_general(qm, k_ref[keys, :], _NT, preferred_element_type=jnp.float32)
        s_scr[n, :, :nk] = s + bias_scr[2 * hp + half, :, A_BAND - nk:]

    def softmax(n):
        keys = keys_of(n // 2)
        nk = keys.stop - keys.start
        s = s_scr[n, :, :nk]
        m = jnp.max(s, axis=1, keepdims=True)
        p_scr[n, :, :nk] = jnp.exp2(s - m).astype(p_scr.dtype)

    def values(n):
        keys = keys_of(n // 2)
        nk = keys.stop - keys.start
        ol = jnp.dot(p_scr[n, :, :nk], vx_scr[keys, :], preferred_element_type=jnp.float32)
        return ol[:, :LANES] / ol[:, LANES:]

    n_stages = 2 * (S // TQ)
    order = list(range(n_stages - 1, -1, -1))
    outs = {}

    def finish(n):
        outs[n] = values(n)
        if n % 2 == 0:
            rows = slice((n // 2) * TQ, (n // 2 + 1) * TQ)
            o_pair = jnp.where(low, outs.pop(n), outs.pop(n + 1))
            gate = g_ref[rows, :].astype(jnp.float32)
            y_ref[rows, :] = (o_pair * (gate * _sigmoid(gate))).astype(y_ref.dtype)

    scores(order[0])
    for i, n in enumerate(order):
        if i + 1 < n_stages:
            scores(order[i + 1])
        softmax(n)
        if i > 0:
            finish(order[i - 1])
    finish(order[-1])


def _attn_a(qkvg, brow):
    B, _, S, _ = qkvg.shape
    return pl.pallas_call(
        _attn_a_kernel,
        grid=(B, N_GROUPS),
        in_specs=[_group_spec(S, 0), _group_spec(S, 1), _group_spec(S, 2), _group_spec(S, 3),
                  pl.BlockSpec(brow.shape, lambda b, h: (0, 0))],
        out_specs=pl.BlockSpec((None, None, S, LANES), lambda b, h: (b, h, 0, 0)),
        out_shape=jax.ShapeDtypeStruct((B, N_GROUPS, S, LANES), jnp.bfloat16),
        scratch_shapes=[
            pltpu.VMEM((A_HEADS, TQ, A_BAND), jnp.float32),
            pltpu.VMEM((S, 2 * LANES), jnp.bfloat16),
            pltpu.VMEM((2 * (S // TQ), TQ, A_BAND), jnp.float32),
            pltpu.VMEM((2 * (S // TQ), TQ, A_BAND), jnp.bfloat16),
        ],
        compiler_params=_ARB2,
        name="attn_a",
    )(qkvg, qkvg, qkvg, qkvg, brow)


def _out_kernel(y_ref, x_ref, w_ref, g_ref, o_ref):
    y = jnp.concatenate([y_ref[gi] for gi in range(N_GROUPS)], axis=1)
    yo = jnp.dot(y, w_ref[...], preferred_element_type=jnp.float32)
    o_ref[...] = x_ref[...] + yo * _rms_scale(yo) * g_ref[...]


def _out_proj(y, x, w, gain, name):
    B, S, _ = x.shape
    return pl.pallas_call(
        _out_kernel,
        grid=(B, S // TM_WIDE),
        in_specs=[
            pl.BlockSpec((None, N_GROUPS, TM_WIDE, LANES), lambda b, i: (b, 0, i, 0)),
            pl.BlockSpec((None, TM_WIDE, D_MODEL), lambda b, i: (b, i, 0)),
            pl.BlockSpec((D_MODEL, D_MODEL), lambda b, i: (0, 0)),
            pl.BlockSpec((1, D_MODEL), lambda b, i: (0, 0)),
        ],
        out_specs=pl.BlockSpec((None, TM_WIDE, D_MODEL), lambda b, i: (b, i, 0)),
        out_shape=jax.ShapeDtypeStruct((B, S, D_MODEL), jnp.float32),
        compiler_params=_ARB2,
        name=name,
    )(y, x, w, gain)


def _rope(t, cos, sin_lo, sin_hi):
    return t * cos + pltpu.roll(t, LANES - ROT_DIM // 2, 1) * sin_lo + pltpu.roll(t, ROT_DIM // 2, 1) * sin_hi


def _proj_b_kernel(y_ref, xin_ref, wout_ref, gpost_ref, pos_ref, freq_ref, cosr_ref, sinr_ref, gkv_ref, gb_ref,
                   wkv_ref, wb_ref, h_ref, o_ref):
    half = TM // 2
    halves = (slice(0, half), slice(half, TM))

    ang0 = pos_ref[...].astype(jnp.float32) * freq_ref[...]
    cos0 = jnp.cos(ang0)
    sin0 = jnp.sin(ang0)
    lane = lax.broadcasted_iota(jnp.int32, (half, LANES), 1) % HEAD_DIM

    def outproj(rows):
        y = jnp.concatenate([y_ref[gi, rows, :] for gi in range(N_GROUPS)], axis=1)
        return jnp.dot(y, wout_ref[...], preferred_element_type=jnp.float32)

    def norms(rows, yo):
        x = xin_ref[rows, :] + yo * _rms_scale(yo) * gpost_ref[...]
        h_ref[rows, :] = x
        xn = x * _rms_scale(x)
        return (xn * gkv_ref[...]).astype(jnp.bfloat16), (xn * gb_ref[...]).astype(jnp.bfloat16)

    def project(rows, u_kv, u_b):
        cos = cos0 * cosr_ref[rows, :] - sin0 * sinr_ref[rows, :]
        sin = sin0 * cosr_ref[rows, :] + cos0 * sinr_ref[rows, :]
        sin_lo = jnp.where(lane < ROT_DIM // 2, -sin, 0.0)
        sin_hi = jnp.where((lane >= ROT_DIM // 2) & (lane < ROT_DIM), sin, 0.0)

        def store(base, r, rope, scale=1.0):
            for gi in range(N_GROUPS):
                t = r[:, gi * LANES:(gi + 1) * LANES]
                if rope:
                    t = _rope(t, cos, sin_lo, sin_hi)
                if scale != 1.0:
                    t = t * scale
                o_ref[base + gi, rows, :] = t.astype(o_ref.dtype)

        store(0, jnp.dot(u_kv, wkv_ref[:, :D_MODEL], preferred_element_type=jnp.float32), True)
        store(N_GROUPS, jnp.dot(u_kv, wkv_ref[:, D_MODEL:], preferred_element_type=jnp.float32), False)
        store(2 * N_GROUPS, jnp.dot(u_b, wb_ref[:, :D_MODEL], preferred_element_type=jnp.float32), True, Q_SCALE)
        store(3 * N_GROUPS, jnp.dot(u_b, wb_ref[:, D_MODEL:], preferred_element_type=jnp.float32), False)

    yo0 = outproj(halves[0])
    yo1 = outproj(halves[1])
    u0 = norms(halves[0], yo0)
    project(halves[0], *u0)
    u1 = norms(halves[1], yo1)
    project(halves[1], *u1)


def _proj_b(y, x, wout, gpost, pos, freq, gkv, gb, wkv, wb):
    B, S, _ = x.shape
    row_ang = jnp.arange(TM, dtype=jnp.float32)[:, None] * freq
    cosr, sinr = jnp.cos(row_ang), jnp.sin(row_ang)
    return pl.pallas_call(
        _proj_b_kernel,
        grid=(B, S // TM),
        in_specs=[
            pl.BlockSpec((None, N_GROUPS, TM, LANES), lambda b, i: (b, 0, i, 0)),
            pl.BlockSpec((None, TM, D_MODEL), lambda b, i: (b, i, 0)),
            pl.BlockSpec((D_MODEL, D_MODEL), lambda b, i: (0, 0)),
            pl.BlockSpec((1, D_MODEL), lambda b, i: (0, 0)),
            pl.BlockSpec((None, None, 1, 1), lambda b, i: (b, i, 0, 0)),
            pl.BlockSpec((1, LANES), lambda b, i: (0, 0)),
            pl.BlockSpec((TM, LANES), lambda b, i: (0, 0)),
            pl.BlockSpec((TM, LANES), lambda b, i: (0, 0)),
            pl.BlockSpec((1, D_MODEL), lambda b, i: (0, 0)),
            pl.BlockSpec((1, D_MODEL), lambda b, i: (0, 0)),
            pl.BlockSpec((D_MODEL, 2 * D_MODEL), lambda b, i: (0, 0)),
            pl.BlockSpec((D_MODEL, 2 * D_MODEL), lambda b, i: (0, 0)),
        ],
        out_specs=[
            pl.BlockSpec((None, TM, D_MODEL), lambda b, i: (b, i, 0)),
            pl.BlockSpec((None, 4 * N_GROUPS, TM, LANES), lambda b, i: (b, 0, i, 0)),
        ],
        out_shape=[
            jax.ShapeDtypeStruct((B, S, D_MODEL), jnp.float32),
            jax.ShapeDtypeStruct((B, 4 * N_GROUPS, S, LANES), jnp.bfloat16),
        ],
        compiler_params=_ARB2,
        name="proj_b",
    )(y, x, wout, gpost, pos, freq, cosr, sinr, gkv, gb, wkv, wb)


def _attn_b_kernel(k_ref, v_ref, q_ref, g_ref, lam_ref, subln_ref, y_ref, vx_scr, *stage_scr):
    S = q_ref.shape[0]
    n_q = S // TQ
    s_scr, p_scr = stage_scr[:n_q], stage_scr[n_q:]
    lp = lam_ref[...]
    lam = (jnp.exp(jnp.sum(lp[0:1] * lp[1:2], axis=1, keepdims=True))
           - jnp.exp(jnp.sum(lp[2:3] * lp[3:4], axis=1, keepdims=True)) + LAM_INIT_B)

    vx_scr[:, :LANES] = v_ref[...]
    vx_scr[:, LANES:] = jnp.ones((S, LANES), vx_scr.dtype)

    low = lax.broadcasted_iota(jnp.int32, (TQ, LANES), 1) < HEAD_DIM
    q_chunk = lax.broadcasted_iota(jnp.int32, (2 * TQ, TQ), 0) % TQ // CHUNK
    k_chunk = lax.broadcasted_iota(jnp.int32, (2 * TQ, TQ), 1) // CHUNK
    diag_mask = k_chunk <= q_chunk

    def scores(qi):
        q2 = q_ref[qi * TQ:(qi + 1) * TQ, :]
        zero = jnp.zeros_like(q2)
        qs = jnp.concatenate([jnp.where(low, q2, zero), jnp.where(low, zero, q2)], axis=0)
        if qi > 0:
            s_scr[qi][:, :qi * TQ] = lax.dot_general(qs, k_ref[:qi * TQ, :], _NT,
                                                     preferred_element_type=jnp.float32)
        s_d = lax.dot_general(qs, k_ref[qi * TQ:(qi + 1) * TQ, :], _NT, preferred_element_type=jnp.float32)
        s_scr[qi][:, qi * TQ:] = jnp.where(diag_mask, s_d, NEG_INF)

    def softmax(qi):
        s = s_scr[qi][...]
        m = jnp.max(s, axis=1, keepdims=True)
        p_scr[qi][...] = jnp.exp2(s - m).astype(p_scr[qi].dtype)

    def values(qi):
        nk = (qi + 1) * TQ
        rows = slice(qi * TQ, (qi + 1) * TQ)
        ol = jnp.dot(p_scr[qi][...], vx_scr[:nk, :], preferred_element_type=jnp.float32)
        o = ol[:TQ, :LANES] / ol[:TQ, LANES:] - lam * (ol[TQ:, :LANES] / ol[TQ:, LANES:])
        o = o * _rms_scale(o) * subln_ref[...] * (1.0 - LAM_INIT_B)
        gate = g_ref[rows, :].astype(jnp.float32)
        y_ref[rows, :] = (o * (gate * _sigmoid(gate))).astype(y_ref.dtype)

    order = list(range(n_q - 1, -1, -1))
    scores(order[0])
    for n, qi in enumerate(order):
        if n + 1 < n_q:
            scores(order[n + 1])
        softmax(qi)
        if n > 0:
            values(order[n - 1])
    values(order[-1])


def _attn_b(kvqg, lam_rows, subln):
    B, _, S, _ = kvqg.shape
    n_q = S // TQ
    return pl.pallas_call(
        _attn_b_kernel,
        grid=(B, B_HEADS),
        in_specs=[_group_spec(S, 0), _group_spec(S, 1), _group_spec(S, 2), _group_spec(S, 3),
                  pl.BlockSpec(lam_rows.shape, lambda b, h: (0, 0)),
                  pl.BlockSpec((1, LANES), lambda b, h: (0, 0))],
        out_specs=pl.BlockSpec((None, None, S, LANES), lambda b, h: (b, h, 0, 0)),
        out_shape=jax.ShapeDtypeStruct((B, B_HEADS, S, LANES), jnp.bfloat16),
        scratch_shapes=(
            [pltpu.VMEM((S, 2 * LANES), jnp.bfloat16)]
            + [pltpu.VMEM((2 * TQ, (qi + 1) * TQ), jnp.float32) for qi in range(n_q)]
            + [pltpu.VMEM((2 * TQ, (qi + 1) * TQ), jnp.bfloat16) for qi in range(n_q)]),
        compiler_params=_ARB2,
        name="attn_b",
    )(kvqg, kvqg, kvqg, kvqg, lam_rows, subln)


def kernel(x, positions, a_norm_pre, a_w_in, a_rel_bias, a_w_out, a_norm_post, kv_norm, kv_w,
           b_norm_pre, b_w_in, b_lambda_q1, b_lambda_k1, b_lambda_q2, b_lambda_k2, b_subln, b_w_out,
           b_norm_post):
    assert a_w_in.shape[0] == 1 and b_w_in.shape[0] == 1
    bf16 = jnp.bfloat16

    m = np.arange(A_BAND + TQ)
    idx = np.clip(A_BAND - m, -MAX_REL, MAX_REL) + MAX_REL
    brow = jnp.take(a_rel_bias[0], jnp.asarray(idx, jnp.int32), axis=1)

    half = ROT_DIM // 2
    inv_freq = jnp.power(jnp.float32(ROPE_THETA), -jnp.arange(half, dtype=jnp.float32) * 2.0 / ROT_DIM)
    head_freq = jnp.concatenate([inv_freq, inv_freq, jnp.zeros((HEAD_DIM - ROT_DIM,), jnp.float32)])
    freq = jnp.tile(head_freq, LANES // HEAD_DIM)[None, :]

    lam_rows = jnp.zeros((8, LANES), jnp.float32)
    lam_rows = lam_rows.at[0:4, :HEAD_DIM].set(
        jnp.stack([b_lambda_q1[0], b_lambda_k1[0], b_lambda_q2[0], b_lambda_k2[0]]).astype(jnp.float32))

    qkvg = _proj_a(x, a_norm_pre[0][None, :], a_w_in[0].astype(bf16))
    y_a = _attn_a(qkvg, brow)
    tile_pos = positions[:, ::TM, None, None]
    h1, kvqg = _proj_b(y_a, x, a_w_out[0].astype(bf16), a_norm_post[0][None, :], tile_pos, freq,
                       kv_norm[None, :], b_norm_pre[0][None, :], kv_w.astype(bf16), b_w_in[0].astype(bf16))
    y_b = _attn_b(kvqg, lam_rows, b_subln[0][None, :])
    return _out_proj(y_b, h1, b_w_out[0].astype(bf16), b_norm_post[0][None, :], "out_b")
```

```python
import math

import jax
import jax.numpy as jnp
import numpy as np
from jax import lax
from jax.experimental import pallas as pl
from jax.experimental.pallas import tpu as pltpu

D_MODEL = 1024
CHUNK = 64
PAST_CHUNKS = 8
HEAD_DIM = 64
A_HEADS = 16
B_HEADS = 8
MAX_REL = 128
ROT_DIM = 16
ROPE_THETA = 500000.0
RMS_EPS = 1e-6
NEG_INF = -1e30
LAM_INIT_B = 0.8 - 0.6 * math.exp(-0.3 * 1)
LOG2E = math.log2(math.e)
Q_SCALE = LOG2E / math.sqrt(HEAD_DIM)

LANES = 128
N_GROUPS = D_MODEL // LANES
TM = 512
TM_WIDE = 1024
TQ = 256
A_BAND = PAST_CHUNKS * CHUNK + TQ
VMEM_LIMIT = 56 * 1024 * 1024

_NT = (((1,), (1,)), ((), ()))
_ARB2 = pltpu.CompilerParams(dimension_semantics=("arbitrary", "arbitrary"),
                             vmem_limit_bytes=VMEM_LIMIT)


def _rms_scale(x):
    return lax.rsqrt(jnp.mean(x * x, axis=-1, keepdims=True) + RMS_EPS)


def _sigmoid(x):
    return 1.0 / (1.0 + jnp.exp(-x))


def _store_groups(o_ref, base, r):
    for gi in range(N_GROUPS):
        o_ref[base + gi] = r[:, gi * LANES:(gi + 1) * LANES].astype(o_ref.dtype)


def _group_spec(S, slab):
    return pl.BlockSpec((None, None, S, LANES), lambda b, h: (b, slab * N_GROUPS + h, 0, 0))


def _proj_a_kernel(x_ref, g_ref, w_ref, o_ref):
    x = x_ref[...]
    y = (x * _rms_scale(x) * g_ref[...]).astype(jnp.bfloat16)
    for c in (3, 0, 1, 2):
        r = jnp.dot(y, w_ref[:, c * D_MODEL:(c + 1) * D_MODEL], preferred_element_type=jnp.float32)
        if c == 0:
            r = r * Q_SCALE
        if c == 3:
            r = r * _sigmoid(r)
        _store_groups(o_ref, c * N_GROUPS, r)


def _proj_a(x, gain, w):
    B, S, _ = x.shape
    return pl.pallas_call(
        _proj_a_kernel,
        grid=(B, S // TM_WIDE),
        in_specs=[
            pl.BlockSpec((None, TM_WIDE, D_MODEL), lambda b, i: (b, i, 0)),
            pl.BlockSpec((1, D_MODEL), lambda b, i: (0, 0)),
            pl.BlockSpec((D_MODEL, 4 * D_MODEL), lambda b, i: (0, 0)),
        ],
        out_specs=pl.BlockSpec((None, 4 * N_GROUPS, TM_WIDE, LANES), lambda b, i: (b, 0, i, 0)),
        out_shape=jax.ShapeDtypeStruct((B, 4 * N_GROUPS, S, LANES), jnp.bfloat16),
        compiler_params=_ARB2,
        name="proj_a",
    )(x, gain, w)


def _attn_a_kernel(q_ref, k_ref, v_ref, g_ref, brow_ref, y_ref, bias_scr, vx_scr, s_scr, p_scr):
    b = pl.program_id(0)
    hp = pl.program_id(1)
    S = q_ref.shape[0]

    @pl.when((b == 0) & (hp == 0))
    def _build_bias():
        r_chunk = lax.broadcasted_iota(jnp.int32, (TQ, A_BAND), 0) // CHUNK
        j_chunk = lax.broadcasted_iota(jnp.int32, (TQ, A_BAND), 1) // CHUNK
        in_band = (j_chunk >= r_chunk) & (j_chunk <= r_chunk + PAST_CHUNKS)
        for h in range(A_HEADS):
            row = jnp.broadcast_to(brow_ref[h:h + 1, :], (TQ, brow_ref.shape[1]))
            toep = pltpu.roll(row, 0, 1, stride=1, stride_axis=0)
            bias_scr[h] = jnp.where(in_band, toep[:, TQ:] * LOG2E, NEG_INF)

    vx_scr[:, :LANES] = v_ref[...]
    vx_scr[:, LANES:] = jnp.ones((S, LANES), vx_scr.dtype)

    low = lax.broadcasted_iota(jnp.int32, (TQ, LANES), 1) < HEAD_DIM

    def keys_of(qi):
        return slice(max(0, (qi + 1) * TQ - A_BAND), (qi + 1) * TQ)

    def scores(n):
        qi, half = divmod(n, 2)
        keys = keys_of(qi)
        nk = keys.stop - keys.start
        q2 = q_ref[qi * TQ:(qi + 1) * TQ, :]
        qm = jnp.where(low if half == 0 else ~low, q2, jnp.zeros_like(q2))
        s = lax.dot_general(qm, k_ref[keys, :], _NT, preferred_element_type=jnp.float32)
        s_scr[n, :, :nk] = s + bias_scr[2 * hp + half, :, A_BAND - nk:]

    def softmax(n):
        keys = keys_of(n // 2)
        nk = keys.stop - keys.start
        s = s_scr[n, :, :nk]
        m = jnp.max(s, axis=1, keepdims=True)
        p_scr[n, :, :nk] = jnp.exp2(s - m).astype(p_scr.dtype)

    def values(n):
        keys = keys_of(n // 2)
        nk = keys.stop - keys.start
        ol = jnp.dot(p_scr[n, :, :nk], vx_scr[keys, :], preferred_element_type=jnp.float32)
        return ol[:, :LANES] / ol[:, LANES:]

    n_stages = 2 * (S // TQ)
    order = list(range(n_stages - 1, -1, -1))
    outs = {}

    def finish(n):
        outs[n] = values(n)
        if n % 2 == 0:
            rows = slice((n // 2) * TQ, (n // 2 + 1) * TQ)
            o_pair = jnp.where(low, outs.pop(n), outs.pop(n + 1))
            y_ref[rows, :] = (o_pair * g_ref[rows, :].astype(jnp.float32)).astype(y_ref.dtype)

    scores(order[0])
    for i, n in enumerate(order):
        if i + 1 < n_stages:
            scores(order[i + 1])
        softmax(n)
        if i > 0:
            finish(order[i - 1])
    finish(order[-1])


def _attn_a(qkvg, brow):
    B, _, S, _ = qkvg.shape
    return pl.pallas_call(
        _attn_a_kernel,
        grid=(B, N_GROUPS),
        in_specs=[_group_spec(S, 0), _group_spec(S, 1), _group_spec(S, 2), _group_spec(S, 3),
                  pl.BlockSpec(brow.shape, lambda b, h: (0, 0))],
        out_specs=pl.BlockSpec((None, None, S, LANES), lambda b, h: (b, h, 0, 0)),
        out_shape=jax.ShapeDtypeStruct((B, N_GROUPS, S, LANES), jnp.bfloat16),
        scratch_shapes=[
            pltpu.VMEM((A_HEADS, TQ, A_BAND), jnp.float32),
            pltpu.VMEM((S, 2 * LANES), jnp.bfloat16),
            pltpu.VMEM((2 * (S // TQ), TQ, A_BAND), jnp.float32),
            pltpu.VMEM((2 * (S // TQ), TQ, A_BAND), jnp.bfloat16),
        ],
        compiler_params=_ARB2,
        name="attn_a",
    )(qkvg, qkvg, qkvg, qkvg, brow)


def _out_kernel(y_ref, x_ref, w_ref, g_ref, o_ref):
    y = jnp.concatenate([y_ref[gi] for gi in range(N_GROUPS)], axis=1)
    yo = jnp.dot(y, w_ref[...], preferred_element_type=jnp.float32)
    o_ref[...] = x_ref[...] + yo * _rms_scale(yo) * g_ref[...]


def _out_proj(y, x, w, gain, name):
    B, S, _ = x.shape
    return pl.pallas_call(
        _out_kernel,
        grid=(B, S // TM_WIDE),
        in_specs=[
            pl.BlockSpec((None, N_GROUPS, TM_WIDE, LANES), lambda b, i: (b, 0, i, 0)),
            pl.BlockSpec((None, TM_WIDE, D_MODEL), lambda b, i: (b, i, 0)),
            pl.BlockSpec((D_MODEL, D_MODEL), lambda b, i: (0, 0)),
            pl.BlockSpec((1, D_MODEL), lambda b, i: (0, 0)),
        ],
        out_specs=pl.BlockSpec((None, TM_WIDE, D_MODEL), lambda b, i: (b, i, 0)),
        out_shape=jax.ShapeDtypeStruct((B, S, D_MODEL), jnp.float32),
        compiler_params=_ARB2,
        name=name,
    )(y, x, w, gain)


def _rope(t, cos, sin_lo, sin_hi):
    return t * cos + pltpu.roll(t, LANES - ROT_DIM // 2, 1) * sin_lo + pltpu.roll(t, ROT_DIM // 2, 1) * sin_hi


def _proj_b_kernel(y_ref, xin_ref, wout_ref, gpost_ref, pos_ref, freq_ref, cosr_ref, sinr_ref, gkv_ref, gb_ref,
                   wkv_ref, wb_ref, h_ref, o_ref):
    half = TM // 2
    halves = (slice(0, half), slice(half, TM))

    ang0 = pos_ref[...].astype(jnp.float32) * freq_ref[...]
    cos0 = jnp.cos(ang0)
    sin0 = jnp.sin(ang0)
    lane = lax.broadcasted_iota(jnp.int32, (half, LANES), 1) % HEAD_DIM

    def outproj(rows):
        y = jnp.concatenate([y_ref[gi, rows, :] for gi in range(N_GROUPS)], axis=1)
        return jnp.dot(y, wout_ref[...], preferred_element_type=jnp.float32)

    def norms(rows, yo):
        x = xin_ref[rows, :] + yo * _rms_scale(yo) * gpost_ref[...]
        h_ref[rows, :] = x
        xn = x * _rms_scale(x)
        return (xn * gkv_ref[...]).astype(jnp.bfloat16), (xn * gb_ref[...]).astype(jnp.bfloat16)

    def project(rows, u_kv, u_b):
        cos = cos0 * cosr_ref[rows, :] - sin0 * sinr_ref[rows, :]
        sin = sin0 * cosr_ref[rows, :] + cos0 * sinr_ref[rows, :]
        sin_lo = jnp.where(lane < ROT_DIM // 2, -sin, 0.0)
        sin_hi = jnp.where((lane >= ROT_DIM // 2) & (lane < ROT_DIM), sin, 0.0)

        def store(base, r, rope, scale=1.0):
            for gi in range(N_GROUPS):
                t = r[:, gi * LANES:(gi + 1) * LANES]
                if rope:
                    t = _rope(t, cos, sin_lo, sin_hi)
                if scale != 1.0:
                    t = t * scale
                o_ref[base + gi, rows, :] = t.astype(o_ref.dtype)

        g = jnp.dot(u_b, wb_ref[:, D_MODEL:], preferred_element_type=jnp.float32)
        store(3 * N_GROUPS, g * _sigmoid(g), False)
        store(0, jnp.dot(u_kv, wkv_ref[:, :D_MODEL], preferred_element_type=jnp.float32), True)
        store(2 * N_GROUPS, jnp.dot(u_b, wb_ref[:, :D_MODEL], preferred_element_type=jnp.float32), True, Q_SCALE)
        store(N_GROUPS, jnp.dot(u_kv, wkv_ref[:, D_MODEL:], preferred_element_type=jnp.float32), False)

    yo0 = outproj(halves[0])
    yo1 = outproj(halves[1])
    u0 = norms(halves[0], yo0)
    project(halves[0], *u0)
    u1 = norms(halves[1], yo1)
    project(halves[1], *u1)


def _proj_b(y, x, wout, gpost, pos, freq, gkv, gb, wkv, wb):
    B, S, _ = x.shape
    row_ang = jnp.arange(TM, dtype=jnp.float32)[:, None] * freq
    cosr, sinr = jnp.cos(row_ang), jnp.sin(row_ang)
    return pl.pallas_call(
        _proj_b_kernel,
        grid=(B, S // TM),
        in_specs=[
            pl.BlockSpec((None, N_GROUPS, TM, LANES), lambda b, i: (b, 0, i, 0)),
            pl.BlockSpec((None, TM, D_MODEL), lambda b, i: (b, i, 0)),
            pl.BlockSpec((D_MODEL, D_MODEL), lambda b, i: (0, 0)),
            pl.BlockSpec((1, D_MODEL), lambda b, i: (0, 0)),
            pl.BlockSpec((None, None, 1, 1), lambda b, i: (b, i, 0, 0)),
            pl.BlockSpec((1, LANES), lambda b, i: (0, 0)),
            pl.BlockSpec((TM, LANES), lambda b, i: (0, 0)),
            pl.BlockSpec((TM, LANES), lambda b, i: (0, 0)),
            pl.BlockSpec((1, D_MODEL), lambda b, i: (0, 0)),
            pl.BlockSpec((1, D_MODEL), lambda b, i: (0, 0)),
            pl.BlockSpec((D_MODEL, 2 * D_MODEL), lambda b, i: (0, 0)),
            pl.BlockSpec((D_MODEL, 2 * D_MODEL), lambda b, i: (0, 0)),
        ],
        out_specs=[
            pl.BlockSpec((None, TM, D_MODEL), lambda b, i: (b, i, 0)),
            pl.BlockSpec((None, 4 * N_GROUPS, TM, LANES), lambda b, i: (b, 0, i, 0)),
        ],
        out_shape=[
            jax.ShapeDtypeStruct((B, S, D_MODEL), jnp.float32),
            jax.ShapeDtypeStruct((B, 4 * N_GROUPS, S, LANES), jnp.bfloat16),
        ],
        compiler_params=_ARB2,
        name="proj_b",
    )(y, x, wout, gpost, pos, freq, cosr, sinr, gkv, gb, wkv, wb)


def _attn_b_kernel(k_ref, v_ref, q_ref, g_ref, lam_ref, subln_ref, y_ref, vx_scr, *stage_scr):
    S = q_ref.shape[0]
    n_q = S // TQ
    s_scr, p_scr = stage_scr[:n_q], stage_scr[n_q:]
    lp = lam_ref[...]
    lam = (jnp.exp(jnp.sum(lp[0:1] * lp[1:2], axis=1, keepdims=True))
           - jnp.exp(jnp.sum(lp[2:3] * lp[3:4], axis=1, keepdims=True)) + LAM_INIT_B)

    vx_scr[:, :LANES] = v_ref[...]
    vx_scr[:, LANES:] = jnp.ones((S, LANES), vx_scr.dtype)

    low = lax.broadcasted_iota(jnp.int32, (TQ, LANES), 1) < HEAD_DIM
    q_chunk = lax.broadcasted_iota(jnp.int32, (2 * TQ, TQ), 0) % TQ // CHUNK
    k_chunk = lax.broadcasted_iota(jnp.int32, (2 * TQ, TQ), 1) // CHUNK
    diag_mask = k_chunk <= q_chunk

    def scores(qi):
        q2 = q_ref[qi * TQ:(qi + 1) * TQ, :]
        zero = jnp.zeros_like(q2)
        qs = jnp.concatenate([jnp.where(low, q2, zero), jnp.where(low, zero, q2)], axis=0)
        if qi > 0:
            s_scr[qi][:, :qi * TQ] = lax.dot_general(qs, k_ref[:qi * TQ, :], _NT,
                                                     preferred_element_type=jnp.float32)
        s_d = lax.dot_general(qs, k_ref[qi * TQ:(qi + 1) * TQ, :], _NT, preferred_element_type=jnp.float32)
        s_scr[qi][:, qi * TQ:] = jnp.where(diag_mask, s_d, NEG_INF)

    def softmax(qi):
        s = s_scr[qi][...]
        m = jnp.max(s, axis=1, keepdims=True)
        p_scr[qi][...] = jnp.exp2(s - m).astype(p_scr[qi].dtype)

    def values(qi):
        nk = (qi + 1) * TQ
        rows = slice(qi * TQ, (qi + 1) * TQ)
        ol = jnp.dot(p_scr[qi][...], vx_scr[:nk, :], preferred_element_type=jnp.float32)
        o = ol[:TQ, :LANES] / ol[:TQ, LANES:] - lam * (ol[TQ:, :LANES] / ol[TQ:, LANES:])
        o = o * _rms_scale(o) * subln_ref[...] * (1.0 - LAM_INIT_B)
        y_ref[rows, :] = (o * g_ref[rows, :].astype(jnp.float32)).astype(y_ref.dtype)

    order = list(range(n_q - 1, -1, -1))
    scores(order[0])
    for n, qi in enumerate(order):
        if n + 1 < n_q:
            scores(order[n + 1])
        softmax(qi)
        if n > 0:
            values(order[n - 1])
    values(order[-1])


def _attn_b(kvqg, lam_rows, subln):
    B, _, S, _ = kvqg.shape
    n_q = S // TQ
    return pl.pallas_call(
        _attn_b_kernel,
        grid=(B, B_HEADS),
        in_specs=[_group_spec(S, 0), _group_spec(S, 1), _group_spec(S, 2), _group_spec(S, 3),
                  pl.BlockSpec(lam_rows.shape, lambda b, h: (0, 0)),
                  pl.BlockSpec((1, LANES), lambda b, h: (0, 0))],
        out_specs=pl.BlockSpec((None, None, S, LANES), lambda b, h: (b, h, 0, 0)),
        out_shape=jax.ShapeDtypeStruct((B, B_HEADS, S, LANES), jnp.bfloat16),
        scratch_shapes=(
            [pltpu.VMEM((S, 2 * LANES), jnp.bfloat16)]
            + [pltpu.VMEM((2 * TQ, (qi + 1) * TQ), jnp.float32) for qi in range(n_q)]
            + [pltpu.VMEM((2 * TQ, (qi + 1) * TQ), jnp.bfloat16) for qi in range(n_q)]),
        compiler_params=_ARB2,
        name="attn_b",
    )(kvqg, kvqg, kvqg, kvqg, lam_rows, subln)


def kernel(x, positions, a_norm_pre, a_w_in, a_rel_bias, a_w_out, a_norm_post, kv_norm, kv_w,
           b_norm_pre, b_w_in, b_lambda_q1, b_lambda_k1, b_lambda_q2, b_lambda_k2, b_subln, b_w_out,
           b_norm_post):
    assert a_w_in.shape[0] == 1 and b_w_in.shape[0] == 1
    bf16 = jnp.bfloat16

    m = np.arange(A_BAND + TQ)
    idx = np.clip(A_BAND - m, -MAX_REL, MAX_REL) + MAX_REL
    brow = jnp.take(a_rel_bias[0], jnp.asarray(idx, jnp.int32), axis=1)

    half = ROT_DIM // 2
    inv_freq = jnp.power(jnp.float32(ROPE_THETA), -jnp.arange(half, dtype=jnp.float32) * 2.0 / ROT_DIM)
    head_freq = jnp.concatenate([inv_freq, inv_freq, jnp.zeros((HEAD_DIM - ROT_DIM,), jnp.float32)])
    freq = jnp.tile(head_freq, LANES // HEAD_DIM)[None, :]

    lam_rows = jnp.zeros((8, LANES), jnp.float32)
    lam_rows = lam_rows.at[0:4, :HEAD_DIM].set(
        jnp.stack([b_lambda_q1[0], b_lambda_k1[0], b_lambda_q2[0], b_lambda_k2[0]]).astype(jnp.float32))

    qkvg = _proj_a(x, a_norm_pre[0][None, :], a_w_in[0].astype(bf16))
    y_a = _attn_a(qkvg, brow)
    tile_pos = positions[:, ::TM, None, None]
    h1, kvqg = _proj_b(y_a, x, a_w_out[0].astype(bf16), a_norm_post[0][None, :], tile_pos, freq,
                       kv_norm[None, :], b_norm_pre[0][None, :], kv_w.astype(bf16), b_w_in[0].astype(bf16))
    y_b = _attn_b(kvqg, lam_rows, b_subln[0][None, :])
    return _out_proj(y_b, h1, b_w_out[0].astype(bf16), b_norm_post[0][None, :], "out_b")
```

```python
import math

import jax
import jax.numpy as jnp
import numpy as np
from jax import lax
from jax.experimental import pallas as pl
from jax.experimental.pallas import tpu as pltpu

D_MODEL = 1024
CHUNK = 64
PAST_CHUNKS = 8
HEAD_DIM = 64
A_HEADS = 16
B_HEADS = 8
MAX_REL = 128
ROT_DIM = 16
ROPE_THETA = 500000.0
RMS_EPS = 1e-6
NEG_INF = -1e30
LAM_INIT_B = 0.8 - 0.6 * math.exp(-0.3 * 1)
LOG2E = math.log2(math.e)
Q_SCALE = LOG2E / math.sqrt(HEAD_DIM)

LANES = 128
N_GROUPS = D_MODEL // LANES
TM = 512
TM_WIDE = 1024
TQ = 256
A_BAND = PAST_CHUNKS * CHUNK + TQ
VMEM_LIMIT = 56 * 1024 * 1024

_NT = (((1,), (1,)), ((), ()))
_ARB2 = pltpu.CompilerParams(dimension_semantics=("arbitrary", "arbitrary"),
                             vmem_limit_bytes=VMEM_LIMIT)


def _rms_scale(x):
    return lax.rsqrt(jnp.mean(x * x, axis=-1, keepdims=True) + RMS_EPS)


def _sigmoid(x):
    return 1.0 / (1.0 + jnp.exp(-x))


def _store_groups(o_ref, base, r):
    for gi in range(N_GROUPS):
        o_ref[base + gi] = r[:, gi * LANES:(gi + 1) * LANES].astype(o_ref.dtype)


def _group_spec(S, slab):
    return pl.BlockSpec((None, None, S, LANES), lambda b, h: (b, slab * N_GROUPS + h, 0, 0))


def _proj_a_kernel(x_ref, g_ref, w_ref, o_ref):
    x = x_ref[...]
    y = (x * _rms_scale(x) * g_ref[...]).astype(jnp.bfloat16)
    for c in range(4):
        r = jnp.dot(y, w_ref[:, c * D_MODEL:(c + 1) * D_MODEL], preferred_element_type=jnp.float32)
        if c == 0:
            r = r * Q_SCALE
        _store_groups(o_ref, c * N_GROUPS, r)


def _proj_a(x, gain, w):
    B, S, _ = x.shape
    return pl.pallas_call(
        _proj_a_kernel,
        grid=(B, S // TM_WIDE),
        in_specs=[
            pl.BlockSpec((None, TM_WIDE, D_MODEL), lambda b, i: (b, i, 0)),
            pl.BlockSpec((1, D_MODEL), lambda b, i: (0, 0)),
            pl.BlockSpec((D_MODEL, 4 * D_MODEL), lambda b, i: (0, 0)),
        ],
        out_specs=pl.BlockSpec((None, 4 * N_GROUPS, TM_WIDE, LANES), lambda b, i: (b, 0, i, 0)),
        out_shape=jax.ShapeDtypeStruct((B, 4 * N_GROUPS, S, LANES), jnp.bfloat16),
        compiler_params=_ARB2,
        name="proj_a",
    )(x, gain, w)


def _attn_a_kernel(q_ref, k_ref, v_ref, g_ref, brow_ref, y_ref, bias_scr, vx_scr, s_scr, p_scr):
    b = pl.program_id(0)
    hp = pl.program_id(1)
    S = q_ref.shape[0]

    @pl.when((b == 0) & (hp == 0))
    def _build_bias():
        r_chunk = lax.broadcasted_iota(jnp.int32, (TQ, A_BAND), 0) // CHUNK
        j_chunk = lax.broadcasted_iota(jnp.int32, (TQ, A_BAND), 1) // CHUNK
        in_band = (j_chunk >= r_chunk) & (j_chunk <= r_chunk + PAST_CHUNKS)
        for h in range(A_HEADS):
            row = jnp.broadcast_to(brow_ref[h:h + 1, :], (TQ, brow_ref.shape[1]))
            toep = pltpu.roll(row, 0, 1, stride=1, stride_axis=0)
            bias_scr[h] = jnp.where(in_band, toep[:, TQ:] * LOG2E, NEG_INF)

    vx_scr[:, :LANES] = v_ref[...]
    vx_scr[:, LANES:] = jnp.ones((S, LANES), vx_scr.dtype)

    low = lax.broadcasted_iota(jnp.int32, (TQ, LANES), 1) < HEAD_DIM

    def keys_of(qi):
        return slice(max(0, (qi + 1) * TQ - A_BAND), (qi + 1) * TQ)

    def scores(n):
        qi, half = divmod(n, 2)
        keys = keys_of(qi)
        nk = keys.stop - keys.start
        q2 = q_ref[qi * TQ:(qi + 1) * TQ, :]
        qm = jnp.where(low if half == 0 else ~low, q2, jnp.zeros_like(q2))
        s = lax.dot_general(qm, k_ref[keys, :], _NT, preferred_element_type=jnp.float32)
        s_scr[n, :, :nk] = s + bias_scr[2 * hp + half, :, A_BAND - nk:]

    def softmax(n):
        keys = keys_of(n // 2)
        nk = keys.stop - keys.start
        s = s_scr[n, :, :nk]
        m = jnp.max(s, axis=1, keepdims=True)
        p_scr[n, :, :nk] = jnp.exp2(s - m).astype(p_scr.dtype)

    def values(n):
        keys = keys_of(n // 2)
        nk = keys.stop - keys.start
        ol = jnp.dot(p_scr[n, :, :nk], vx_scr[keys, :], preferred_element_type=jnp.float32)
        return ol[:, :LANES] / ol[:, LANES:]

    n_stages = 2 * (S // TQ)
    order = list(range(n_stages - 1, -1, -1))
    outs = {}

    def finish(n):
        outs[n] = values(n)
        if n % 2 == 0:
            rows = slice((n // 2) * TQ, (n // 2 + 1) * TQ)
            o_pair = jnp.where(low, outs.pop(n), outs.pop(n + 1))
            gate = g_ref[rows, :].astype(jnp.float32)
            y_ref[rows, :] = (o_pair * (gate * _sigmoid(gate))).astype(y_ref.dtype)

    scores(order[0])
    for i, n in enumerate(order):
        if i + 1 < n_stages:
            scores(order[i + 1])
        softmax(n)
        if i > 0:
            finish(order[i - 1])
    finish(order[-1])


def _attn_a(qkvg, brow):
    B, _, S, _ = qkvg.shape
    return pl.pallas_call(
        _attn_a_kernel,
        grid=(B, N_GROUPS),
        in_specs=[_group_spec(S, 0), _group_spec(S, 1), _group_spec(S, 2), _group_spec(S, 3),
                  pl.BlockSpec(brow.shape, lambda b, h: (0, 0))],
        out_specs=pl.BlockSpec((None, None, S, LANES), lambda b, h: (b, h, 0, 0)),
        out_shape=jax.ShapeDtypeStruct((B, N_GROUPS, S, LANES), jnp.bfloat16),
        scratch_shapes=[
            pltpu.VMEM((A_HEADS, TQ, A_BAND), jnp.float32),
            pltpu.VMEM((S, 2 * LANES), jnp.bfloat16),
            pltpu.VMEM((2 * (S // TQ), TQ, A_BAND), jnp.float32),
            pltpu.VMEM((2 * (S // TQ), TQ, A_BAND), jnp.bfloat16),
        ],
        compiler_params=_ARB2,
        name="attn_a",
    )(qkvg, qkvg, qkvg, qkvg, brow)


def _out_kernel(y_ref, x_ref, w_ref, g_ref, o_ref):
    y = jnp.concatenate([y_ref[gi] for gi in range(N_GROUPS)], axis=1)
    yo = jnp.dot(y, w_ref[...], preferred_element_type=jnp.float32)
    o_ref[...] = x_ref[...] + yo * _rms_scale(yo) * g_ref[...]


def _out_proj(y, x, w, gain, name):
    B, S, _ = x.shape
    return pl.pallas_call(
        _out_kernel,
        grid=(B, S // TM_WIDE),
        in_specs=[
            pl.BlockSpec((None, N_GROUPS, TM_WIDE, LANES), lambda b, i: (b, 0, i, 0)),
            pl.BlockSpec((None, TM_WIDE, D_MODEL), lambda b, i: (b, i, 0)),
            pl.BlockSpec((D_MODEL, D_MODEL), lambda b, i: (0, 0)),
            pl.BlockSpec((1, D_MODEL), lambda b, i: (0, 0)),
        ],
        out_specs=pl.BlockSpec((None, TM_WIDE, D_MODEL), lambda b, i: (b, i, 0)),
        out_shape=jax.ShapeDtypeStruct((B, S, D_MODEL), jnp.float32),
        compiler_params=_ARB2,
        name=name,
    )(y, x, w, gain)


def _rope(t, cos, sin_lo, sin_hi):
    return t * cos + pltpu.roll(t, LANES - ROT_DIM // 2, 1) * sin_lo + pltpu.roll(t, ROT_DIM // 2, 1) * sin_hi


def _proj_b_kernel(y_ref, xin_ref, wout_ref, gpost_ref, pos_ref, freq_ref, cosr_ref, sinr_ref, gkv_ref, gb_ref,
                   wkv_ref, wb_ref, h_ref, o_ref):
    half = TM // 2
    halves = (slice(0, half), slice(half, TM))

    ang0 = pos_ref[...].astype(jnp.float32) * freq_ref[...]
    cos0 = jnp.cos(ang0)
    sin0 = jnp.sin(ang0)
    lane = lax.broadcasted_iota(jnp.int32, (half, LANES), 1) % HEAD_DIM

    def outproj(rows):
        y = jnp.concatenate([y_ref[gi, rows, :] for gi in range(N_GROUPS)], axis=1)
        return jnp.dot(y, wout_ref[...], preferred_element_type=jnp.float32)

    def norms(rows, yo):
        x = xin_ref[rows, :] + yo * _rms_scale(yo) * gpost_ref[...]
        h_ref[rows, :] = x
        xn = x * _rms_scale(x)
        return (xn * gkv_ref[...]).astype(jnp.bfloat16), (xn * gb_ref[...]).astype(jnp.bfloat16)

    def project(rows, u_kv, u_b):
        cos = cos0 * cosr_ref[rows, :] - sin0 * sinr_ref[rows, :]
        sin = sin0 * cosr_ref[rows, :] + cos0 * sinr_ref[rows, :]
        sin_lo = jnp.where(lane < ROT_DIM // 2, -sin, 0.0)
        sin_hi = jnp.where((lane >= ROT_DIM // 2) & (lane < ROT_DIM), sin, 0.0)

        def store(base, r, rope, scale=1.0):
            for gi in range(N_GROUPS):
                t = r[:, gi * LANES:(gi + 1) * LANES]
                if rope:
                    t = _rope(t, cos, sin_lo, sin_hi)
                if scale != 1.0:
                    t = t * scale
                o_ref[base + gi, rows, :] = t.astype(o_ref.dtype)

        g = jnp.dot(u_b, wb_ref[:, D_MODEL:], preferred_element_type=jnp.float32)
        store(3 * N_GROUPS, g * _sigmoid(g), False)
        store(0, jnp.dot(u_kv, wkv_ref[:, :D_MODEL], preferred_element_type=jnp.float32), True)
        store(2 * N_GROUPS, jnp.dot(u_b, wb_ref[:, :D_MODEL], preferred_element_type=jnp.float32), True, Q_SCALE)
        store(N_GROUPS, jnp.dot(u_kv, wkv_ref[:, D_MODEL:], preferred_element_type=jnp.float32), False)

    yo0 = outproj(halves[0])
    yo1 = outproj(halves[1])
    u0 = norms(halves[0], yo0)
    project(halves[0], *u0)
    u1 = norms(halves[1], yo1)
    project(halves[1], *u1)


def _proj_b(y, x, wout, gpost, pos, freq, gkv, gb, wkv, wb):
    B, S, _ = x.shape
    row_ang = jnp.arange(TM, dtype=jnp.float32)[:, None] * freq
    cosr, sinr = jnp.cos(row_ang), jnp.sin(row_ang)
    return pl.pallas_call(
        _proj_b_kernel,
        grid=(B, S // TM),
        in_specs=[
            pl.BlockSpec((None, N_GROUPS, TM, LANES), lambda b, i: (b, 0, i, 0)),
            pl.BlockSpec((None, TM, D_MODEL), lambda b, i: (b, i, 0)),
            pl.BlockSpec((D_MODEL, D_MODEL), lambda b, i: (0, 0)),
            pl.BlockSpec((1, D_MODEL), lambda b, i: (0, 0)),
            pl.BlockSpec((None, None, 1, 1), lambda b, i: (b, i, 0, 0)),
            pl.BlockSpec((1, LANES), lambda b, i: (0, 0)),
            pl.BlockSpec((TM, LANES), lambda b, i: (0, 0)),
            pl.BlockSpec((TM, LANES), lambda b, i: (0, 0)),
            pl.BlockSpec((1, D_MODEL), lambda b, i: (0, 0)),
            pl.BlockSpec((1, D_MODEL), lambda b, i: (0, 0)),
            pl.BlockSpec((D_MODEL, 2 * D_MODEL), lambda b, i: (0, 0)),
            pl.BlockSpec((D_MODEL, 2 * D_MODEL), lambda b, i: (0, 0)),
        ],
        out_specs=[
            pl.BlockSpec((None, TM, D_MODEL), lambda b, i: (b, i, 0)),
            pl.BlockSpec((None, 4 * N_GROUPS, TM, LANES), lambda b, i: (b, 0, i, 0)),
        ],
        out_shape=[
            jax.ShapeDtypeStruct((B, S, D_MODEL), jnp.float32),
            jax.ShapeDtypeStruct((B, 4 * N_GROUPS, S, LANES), jnp.bfloat16),
        ],
        compiler_params=_ARB2,
        name="proj_b",
    )(y, x, wout, gpost, pos, freq, cosr, sinr, gkv, gb, wkv, wb)


def _attn_b_kernel(k_ref, v_ref, q_ref, g_ref, lam_ref, subln_ref, y_ref, vx_scr, *stage_scr):
    S = q_ref.shape[0]
    n_q = S // TQ
    s_scr, p_scr = stage_scr[:n_q], stage_scr[n_q:]
    lp = lam_ref[...]
    lam = (jnp.exp(jnp.sum(lp[0:1] * lp[1:2], axis=1, keepdims=True))
           - jnp.exp(jnp.sum(lp[2:3] * lp[3:4], axis=1, keepdims=True)) + LAM_INIT_B)

    vx_scr[:, :LANES] = v_ref[...]
    vx_scr[:, LANES:] = jnp.ones((S, LANES), vx_scr.dtype)

    low = lax.broadcasted_iota(jnp.int32, (TQ, LANES), 1) < HEAD_DIM
    q_chunk = lax.broadcasted_iota(jnp.int32, (2 * TQ, TQ), 0) % TQ // CHUNK
    k_chunk = lax.broadcasted_iota(jnp.int32, (2 * TQ, TQ), 1) // CHUNK
    diag_mask = k_chunk <= q_chunk

    def scores(qi):
        q2 = q_ref[qi * TQ:(qi + 1) * TQ, :]
        zero = jnp.zeros_like(q2)
        qs = jnp.concatenate([jnp.where(low, q2, zero), jnp.where(low, zero, q2)], axis=0)
        if qi > 0:
            s_scr[qi][:, :qi * TQ] = lax.dot_general(qs, k_ref[:qi * TQ, :], _NT,
                                                     preferred_element_type=jnp.float32)
        s_d = lax.dot_general(qs, k_ref[qi * TQ:(qi + 1) * TQ, :], _NT, preferred_element_type=jnp.float32)
        s_scr[qi][:, qi * TQ:] = jnp.where(diag_mask, s_d, NEG_INF)

    def softmax(qi):
        s = s_scr[qi][...]
        m = jnp.max(s, axis=1, keepdims=True)
        p_scr[qi][...] = jnp.exp2(s - m).astype(p_scr[qi].dtype)

    def values(qi):
        nk = (qi + 1) * TQ
        rows = slice(qi * TQ, (qi + 1) * TQ)
        ol = jnp.dot(p_scr[qi][...], vx_scr[:nk, :], preferred_element_type=jnp.float32)
        o = ol[:TQ, :LANES] / ol[:TQ, LANES:] - lam * (ol[TQ:, :LANES] / ol[TQ:, LANES:])
        o = o * _rms_scale(o) * subln_ref[...] * (1.0 - LAM_INIT_B)
        y_ref[rows, :] = (o * g_ref[rows, :].astype(jnp.float32)).astype(y_ref.dtype)

    order = list(range(n_q - 1, -1, -1))
    scores(order[0])
    for n, qi in enumerate(order):
        if n + 1 < n_q:
            scores(order[n + 1])
        softmax(qi)
        if n > 0:
            values(order[n - 1])
    values(order[-1])


def _attn_b(kvqg, lam_rows, subln):
    B, _, S, _ = kvqg.shape
    n_q = S // TQ
    return pl.pallas_call(
        _attn_b_kernel,
        grid=(B, B_HEADS),
        in_specs=[_group_spec(S, 0), _group_spec(S, 1), _group_spec(S, 2), _group_spec(S, 3),
                  pl.BlockSpec(lam_rows.shape, lambda b, h: (0, 0)),
                  pl.BlockSpec((1, LANES), lambda b, h: (0, 0))],
        out_specs=pl.BlockSpec((None, None, S, LANES), lambda b, h: (b, h, 0, 0)),
        out_shape=jax.ShapeDtypeStruct((B, B_HEADS, S, LANES), jnp.bfloat16),
        scratch_shapes=(
            [pltpu.VMEM((S, 2 * LANES), jnp.bfloat16)]
            + [pltpu.VMEM((2 * TQ, (qi + 1) * TQ), jnp.float32) for qi in range(n_q)]
            + [pltpu.VMEM((2 * TQ, (qi + 1) * TQ), jnp.bfloat16) for qi in range(n_q)]),
        compiler_params=_ARB2,
        name="attn_b",
    )(kvqg, kvqg, kvqg, kvqg, lam_rows, subln)


def kernel(x, positions, a_norm_pre, a_w_in, a_rel_bias, a_w_out, a_norm_post, kv_norm, kv_w,
           b_norm_pre, b_w_in, b_lambda_q1, b_lambda_k1, b_lambda_q2, b_lambda_k2, b_subln, b_w_out,
           b_norm_post):
    assert a_w_in.shape[0] == 1 and b_w_in.shape[0] == 1
    bf16 = jnp.bfloat16

    m = np.arange(A_BAND + TQ)
    idx = np.clip(A_BAND - m, -MAX_REL, MAX_REL) + MAX_REL
    brow = jnp.take(a_rel_bias[0], jnp.asarray(idx, jnp.int32), axis=1)

    half = ROT_DIM // 2
    inv_freq = jnp.power(jnp.float32(ROPE_THETA), -jnp.arange(half, dtype=jnp.float32) * 2.0 / ROT_DIM)
    head_freq = jnp.concatenate([inv_freq, inv_freq, jnp.zeros((HEAD_DIM - ROT_DIM,), jnp.float32)])
    freq = jnp.tile(head_freq, LANES // HEAD_DIM)[None, :]

    lam_rows = jnp.zeros((8, LANES), jnp.float32)
    lam_rows = lam_rows.at[0:4, :HEAD_DIM].set(
        jnp.stack([b_lambda_q1[0], b_lambda_k1[0], b_lambda_q2[0], b_lambda_k2[0]]).astype(jnp.float32))

    qkvg = _proj_a(x, a_norm_pre[0][None, :], a_w_in[0].astype(bf16))
    y_a = _attn_a(qkvg, brow)
    tile_pos = positions[:, ::TM, None, None]
    h1, kvqg = _proj_b(y_a, x, a_w_out[0].astype(bf16), a_norm_post[0][None, :], tile_pos, freq,
                       kv_norm[None, :], b_norm_pre[0][None, :], kv_w.astype(bf16), b_w_in[0].astype(bf16))
    y_b = _attn_b(kvqg, lam_rows, b_subln[0][None, :])
    return _out_proj(y_b, h1, b_w_out[0].astype(bf16), b_norm_post[0][None, :], "out_b")
```

```python
import math

import jax
import jax.numpy as jnp
import numpy as np
from jax import lax
from jax.experimental import pallas as pl
from jax.experimental.pallas import tpu as pltpu

D_MODEL = 1024
CHUNK = 64
PAST_CHUNKS = 8
HEAD_DIM = 64
A_HEADS = 16
B_HEADS = 8
MAX_REL = 128
ROT_DIM = 16
ROPE_THETA = 500000.0
RMS_EPS = 1e-6
NEG_INF = -1e30
LAM_INIT_B = 0.8 - 0.6 * math.exp(-0.3 * 1)
LOG2E = math.log2(math.e)
Q_SCALE = LOG2E / math.sqrt(HEAD_DIM)

LANES = 128
N_GROUPS = D_MODEL // LANES
TM = 512
TM_WIDE = 1024
TQ = 256
A_BAND = PAST_CHUNKS * CHUNK + TQ
VMEM_LIMIT = 56 * 1024 * 1024

_NT = (((1,), (1,)), ((), ()))
_ARB2 = pltpu.CompilerParams(dimension_semantics=("arbitrary", "arbitrary"),
                             vmem_limit_bytes=VMEM_LIMIT)


def _rms_scale(x):
    return lax.rsqrt(jnp.mean(x * x, axis=-1, keepdims=True) + RMS_EPS)


def _sigmoid(x):
    return 1.0 / (1.0 + jnp.exp(-x))


def _store_groups(o_ref, base, r):
    for gi in range(N_GROUPS):
        o_ref[base + gi] = r[:, gi * LANES:(gi + 1) * LANES].astype(o_ref.dtype)


def _group_spec(S, slab):
    return pl.BlockSpec((None, None, S, LANES), lambda b, h: (b, slab * N_GROUPS + h, 0, 0))


def _proj_a_kernel(x_ref, g_ref, w_ref, o_ref):
    x = x_ref[...]
    y = (x * _rms_scale(x) * g_ref[...]).astype(jnp.bfloat16)
    for c in range(4):
        r = jnp.dot(y, w_ref[:, c * D_MODEL:(c + 1) * D_MODEL], preferred_element_type=jnp.float32)
        if c == 0:
            r = r * Q_SCALE
        _store_groups(o_ref, c * N_GROUPS, r)


def _proj_a(x, gain, w):
    B, S, _ = x.shape
    return pl.pallas_call(
        _proj_a_kernel,
        grid=(B, S // TM_WIDE),
        in_specs=[
            pl.BlockSpec((None, TM_WIDE, D_MODEL), lambda b, i: (b, i, 0)),
            pl.BlockSpec((1, D_MODEL), lambda b, i: (0, 0)),
            pl.BlockSpec((D_MODEL, 4 * D_MODEL), lambda b, i: (0, 0)),
        ],
        out_specs=pl.BlockSpec((None, 4 * N_GROUPS, TM_WIDE, LANES), lambda b, i: (b, 0, i, 0)),
        out_shape=jax.ShapeDtypeStruct((B, 4 * N_GROUPS, S, LANES), jnp.bfloat16),
        compiler_params=_ARB2,
        name="proj_a",
    )(x, gain, w)


def _attn_a_kernel(q_ref, k_ref, v_ref, g_ref, brow_ref, y_ref, bias_scr, vx_scr, s_scr, p_scr):
    b = pl.program_id(0)
    hp = pl.program_id(1)
    S = q_ref.shape[0]

    @pl.when((b == 0) & (hp == 0))
    def _build_bias():
        r_chunk = lax.broadcasted_iota(jnp.int32, (TQ, A_BAND), 0) // CHUNK
        j_chunk = lax.broadcasted_iota(jnp.int32, (TQ, A_BAND), 1) // CHUNK
        in_band = (j_chunk >= r_chunk) & (j_chunk <= r_chunk + PAST_CHUNKS)
        for h in range(A_HEADS):
            row = jnp.broadcast_to(brow_ref[h:h + 1, :], (TQ, brow_ref.shape[1]))
            toep = pltpu.roll(row, 0, 1, stride=1, stride_axis=0)
            bias_scr[h] = jnp.where(in_band, toep[:, TQ:] * LOG2E, NEG_INF)

    vx_scr[:, :LANES] = v_ref[...]
    vx_scr[:, LANES:] = jnp.ones((S, LANES), vx_scr.dtype)

    low = lax.broadcasted_iota(jnp.int32, (TQ, LANES), 1) < HEAD_DIM

    def keys_of(qi):
        return slice(max(0, (qi + 1) * TQ - A_BAND), (qi + 1) * TQ)

    def scores(n):
        qi, half = divmod(n, 2)
        keys = keys_of(qi)
        nk = keys.stop - keys.start
        q2 = q_ref[qi * TQ:(qi + 1) * TQ, :]
        qm = jnp.where(low if half == 0 else ~low, q2, jnp.zeros_like(q2))
        s = lax.dot_general(qm, k_ref[keys, :], _NT, preferred_element_type=jnp.float32)
        s_scr[n, :, :nk] = s + bias_scr[2 * hp + half, :, A_BAND - nk:]

    def softmax(n):
        keys = keys_of(n // 2)
        nk = keys.stop - keys.start
        off = A_BAND - nk
        for rc in range(TQ // CHUNK):
            rows = slice(rc * CHUNK, (rc + 1) * CHUNK)
            lo = max(0, (rc * CHUNK) // LANES * LANES - off)
            hi = min(nk, -(-((rc + PAST_CHUNKS + 1) * CHUNK) // LANES) * LANES - off)
            s = s_scr[n, rows, lo:hi]
            m = jnp.max(s, axis=1, keepdims=True)
            p_scr[n, rows, lo:hi] = jnp.exp2(s - m).astype(p_scr.dtype)
            if lo > 0:
                p_scr[n, rows, :lo] = jnp.zeros((CHUNK, lo), p_scr.dtype)
            if hi < nk:
                p_scr[n, rows, hi:nk] = jnp.zeros((CHUNK, nk - hi), p_scr.dtype)

    def values(n):
        keys = keys_of(n // 2)
        nk = keys.stop - keys.start
        ol = jnp.dot(p_scr[n, :, :nk], vx_scr[keys, :], preferred_element_type=jnp.float32)
        return ol[:, :LANES] / ol[:, LANES:]

    n_stages = 2 * (S // TQ)
    order = list(range(n_stages - 1, -1, -1))
    outs = {}

    def finish(n):
        outs[n] = values(n)
        if n % 2 == 0:
            rows = slice((n // 2) * TQ, (n // 2 + 1) * TQ)
            o_pair = jnp.where(low, outs.pop(n), outs.pop(n + 1))
            gate = g_ref[rows, :].astype(jnp.float32)
            y_ref[rows, :] = (o_pair * (gate * _sigmoid(gate))).astype(y_ref.dtype)

    scores(order[0])
    for i, n in enumerate(order):
        if i + 1 < n_stages:
            scores(order[i + 1])
        softmax(n)
        if i > 0:
            finish(order[i - 1])
    finish(order[-1])


def _attn_a(qkvg, brow):
    B, _, S, _ = qkvg.shape
    return pl.pallas_call(
        _attn_a_kernel,
        grid=(B, N_GROUPS),
        in_specs=[_group_spec(S, 0), _group_spec(S, 1), _group_spec(S, 2), _group_spec(S, 3),
                  pl.BlockSpec(brow.shape, lambda b, h: (0, 0))],
        out_specs=pl.BlockSpec((None, None, S, LANES), lambda b, h: (b, h, 0, 0)),
        out_shape=jax.ShapeDtypeStruct((B, N_GROUPS, S, LANES), jnp.bfloat16),
        scratch_shapes=[
            pltpu.VMEM((A_HEADS, TQ, A_BAND), jnp.float32),
            pltpu.VMEM((S, 2 * LANES), jnp.bfloat16),
            pltpu.VMEM((2 * (S // TQ), TQ, A_BAND), jnp.float32),
            pltpu.VMEM((2 * (S // TQ), TQ, A_BAND), jnp.bfloat16),
        ],
        compiler_params=_ARB2,
        name="attn_a",
    )(qkvg, qkvg, qkvg, qkvg, brow)


def _out_kernel(y_ref, x_ref, w_ref, g_ref, o_ref):
    y = jnp.concatenate([y_ref[gi] for gi in range(N_GROUPS)], axis=1)
    yo = jnp.dot(y, w_ref[...], preferred_element_type=jnp.float32)
    o_ref[...] = x_ref[...] + yo * _rms_scale(yo) * g_ref[...]


def _out_proj(y, x, w, gain, name):
    B, S, _ = x.shape
    return pl.pallas_call(
        _out_kernel,
        grid=(B, S // TM_WIDE),
        in_specs=[
            pl.BlockSpec((None, N_GROUPS, TM_WIDE, LANES), lambda b, i: (b, 0, i, 0)),
            pl.BlockSpec((None, TM_WIDE, D_MODEL), lambda b, i: (b, i, 0)),
            pl.BlockSpec((D_MODEL, D_MODEL), lambda b, i: (0, 0)),
            pl.BlockSpec((1, D_MODEL), lambda b, i: (0, 0)),
        ],
        out_specs=pl.BlockSpec((None, TM_WIDE, D_MODEL), lambda b, i: (b, i, 0)),
        out_shape=jax.ShapeDtypeStruct((B, S, D_MODEL), jnp.float32),
        compiler_params=_ARB2,
        name=name,
    )(y, x, w, gain)


def _rope(t, cos, sin_lo, sin_hi):
    return t * cos + pltpu.roll(t, LANES - ROT_DIM // 2, 1) * sin_lo + pltpu.roll(t, ROT_DIM // 2, 1) * sin_hi


def _proj_b_kernel(y_ref, xin_ref, wout_ref, gpost_ref, pos_ref, freq_ref, cosr_ref, sinr_ref, gkv_ref, gb_ref,
                   wkv_ref, wb_ref, h_ref, o_ref):
    half = TM // 2
    halves = (slice(0, half), slice(half, TM))

    ang0 = pos_ref[...].astype(jnp.float32) * freq_ref[...]
    cos0 = jnp.cos(ang0)
    sin0 = jnp.sin(ang0)
    lane = lax.broadcasted_iota(jnp.int32, (half, LANES), 1) % HEAD_DIM

    def outproj(rows):
        y = jnp.concatenate([y_ref[gi, rows, :] for gi in range(N_GROUPS)], axis=1)
        return jnp.dot(y, wout_ref[...], preferred_element_type=jnp.float32)

    def norms(rows, yo):
        x = xin_ref[rows, :] + yo * _rms_scale(yo) * gpost_ref[...]
        h_ref[rows, :] = x
        xn = x * _rms_scale(x)
        return (xn * gkv_ref[...]).astype(jnp.bfloat16), (xn * gb_ref[...]).astype(jnp.bfloat16)

    def project(rows, u_kv, u_b):
        cos = cos0 * cosr_ref[rows, :] - sin0 * sinr_ref[rows, :]
        sin = sin0 * cosr_ref[rows, :] + cos0 * sinr_ref[rows, :]
        sin_lo = jnp.where(lane < ROT_DIM // 2, -sin, 0.0)
        sin_hi = jnp.where((lane >= ROT_DIM // 2) & (lane < ROT_DIM), sin, 0.0)

        def store(base, r, rope, scale=1.0):
            for gi in range(N_GROUPS):
                t = r[:, gi * LANES:(gi + 1) * LANES]
                if rope:
                    t = _rope(t, cos, sin_lo, sin_hi)
                if scale != 1.0:
                    t = t * scale
                o_ref[base + gi, rows, :] = t.astype(o_ref.dtype)

        g = jnp.dot(u_b, wb_ref[:, D_MODEL:], preferred_element_type=jnp.float32)
        store(3 * N_GROUPS, g * _sigmoid(g), False)
        store(0, jnp.dot(u_kv, wkv_ref[:, :D_MODEL], preferred_element_type=jnp.float32), True)
        store(2 * N_GROUPS, jnp.dot(u_b, wb_ref[:, :D_MODEL], preferred_element_type=jnp.float32), True, Q_SCALE)
        store(N_GROUPS, jnp.dot(u_kv, wkv_ref[:, D_MODEL:], preferred_element_type=jnp.float32), False)

    yo0 = outproj(halves[0])
    yo1 = outproj(halves[1])
    u0 = norms(halves[0], yo0)
    project(halves[0], *u0)
    u1 = norms(halves[1], yo1)
    project(halves[1], *u1)


def _proj_b(y, x, wout, gpost, pos, freq, gkv, gb, wkv, wb):
    B, S, _ = x.shape
    row_ang = jnp.arange(TM, dtype=jnp.float32)[:, None] * freq
    cosr, sinr = jnp.cos(row_ang), jnp.sin(row_ang)
    return pl.pallas_call(
        _proj_b_kernel,
        grid=(B, S // TM),
        in_specs=[
            pl.BlockSpec((None, N_GROUPS, TM, LANES), lambda b, i: (b, 0, i, 0)),
            pl.BlockSpec((None, TM, D_MODEL), lambda b, i: (b, i, 0)),
            pl.BlockSpec((D_MODEL, D_MODEL), lambda b, i: (0, 0)),
            pl.BlockSpec((1, D_MODEL), lambda b, i: (0, 0)),
            pl.BlockSpec((None, None, 1, 1), lambda b, i: (b, i, 0, 0)),
            pl.BlockSpec((1, LANES), lambda b, i: (0, 0)),
            pl.BlockSpec((TM, LANES), lambda b, i: (0, 0)),
            pl.BlockSpec((TM, LANES), lambda b, i: (0, 0)),
            pl.BlockSpec((1, D_MODEL), lambda b, i: (0, 0)),
            pl.BlockSpec((1, D_MODEL), lambda b, i: (0, 0)),
            pl.BlockSpec((D_MODEL, 2 * D_MODEL), lambda b, i: (0, 0)),
            pl.BlockSpec((D_MODEL, 2 * D_MODEL), lambda b, i: (0, 0)),
        ],
        out_specs=[
            pl.BlockSpec((None, TM, D_MODEL), lambda b, i: (b, i, 0)),
            pl.BlockSpec((None, 4 * N_GROUPS, TM, LANES), lambda b, i: (b, 0, i, 0)),
        ],
        out_shape=[
            jax.ShapeDtypeStruct((B, S, D_MODEL), jnp.float32),
            jax.ShapeDtypeStruct((B, 4 * N_GROUPS, S, LANES), jnp.bfloat16),
        ],
        compiler_params=_ARB2,
        name="proj_b",
    )(y, x, wout, gpost, pos, freq, cosr, sinr, gkv, gb, wkv, wb)


def _attn_b_kernel(k_ref, v_ref, q_ref, g_ref, lam_ref, subln_ref, y_ref, vx_scr, *stage_scr):
    S = q_ref.shape[0]
    n_q = S // TQ
    s_scr, p_scr = stage_scr[:n_q], stage_scr[n_q:]
    lp = lam_ref[...]
    lam = (jnp.exp(jnp.sum(lp[0:1] * lp[1:2], axis=1, keepdims=True))
           - jnp.exp(jnp.sum(lp[2:3] * lp[3:4], axis=1, keepdims=True)) + LAM_INIT_B)
    subln_gain = jnp.broadcast_to(subln_ref[...] * (1.0 - LAM_INIT_B), (TQ, LANES))

    vx_scr[:, :LANES] = v_ref[...]
    vx_scr[:, LANES:] = jnp.ones((S, LANES), vx_scr.dtype)

    low = lax.broadcasted_iota(jnp.int32, (TQ, LANES), 1) < HEAD_DIM
    q_chunk = lax.broadcasted_iota(jnp.int32, (2 * TQ, TQ), 0) % TQ // CHUNK
    k_chunk = lax.broadcasted_iota(jnp.int32, (2 * TQ, TQ), 1) // CHUNK
    diag_mask = k_chunk <= q_chunk

    def scores(qi):
        q2 = q_ref[qi * TQ:(qi + 1) * TQ, :]
        zero = jnp.zeros_like(q2)
        qs = jnp.concatenate([jnp.where(low, q2, zero), jnp.where(low, zero, q2)], axis=0)
        if qi > 0:
            s_scr[qi][:, :qi * TQ] = lax.dot_general(qs, k_ref[:qi * TQ, :], _NT,
                                                     preferred_element_type=jnp.float32)
        s_d = lax.dot_general(qs, k_ref[qi * TQ:(qi + 1) * TQ, :], _NT, preferred_element_type=jnp.float32)
        s_scr[qi][:, qi * TQ:] = jnp.where(diag_mask, s_d, NEG_INF)

    def softmax(qi):
        nk = (qi + 1) * TQ
        half = TQ // 2
        for r0 in range(0, 2 * TQ, half):
            rows = slice(r0, r0 + half)
            hi = nk - half if (r0 // half) % 2 == 0 else nk
            s = s_scr[qi][rows, :hi]
            m = jnp.max(s, axis=1, keepdims=True)
            p_scr[qi][rows, :hi] = jnp.exp2(s - m).astype(p_scr[qi].dtype)
            if hi < nk:
                p_scr[qi][rows, hi:] = jnp.zeros((half, nk - hi), p_scr[qi].dtype)

    def values(qi):
        nk = (qi + 1) * TQ
        rows = slice(qi * TQ, (qi + 1) * TQ)
        ol = jnp.dot(p_scr[qi][...], vx_scr[:nk, :], preferred_element_type=jnp.float32)
        o = ol[:TQ, :LANES] / ol[:TQ, LANES:] - lam * (ol[TQ:, :LANES] / ol[TQ:, LANES:])
        o = o * _rms_scale(o) * subln_gain
        y_ref[rows, :] = (o * g_ref[rows, :].astype(jnp.float32)).astype(y_ref.dtype)

    order = list(range(n_q - 1, -1, -1))
    scores(order[0])
    for n, qi in enumerate(order):
        if n + 1 < n_q:
            scores(order[n + 1])
        softmax(qi)
        if n > 0:
            values(order[n - 1])
    values(order[-1])


def _attn_b(kvqg, lam_rows, subln):
    B, _, S, _ = kvqg.shape
    n_q = S // TQ
    return pl.pallas_call(
        _attn_b_kernel,
        grid=(B, B_HEADS),
        in_specs=[_group_spec(S, 0), _group_spec(S, 1), _group_spec(S, 2), _group_spec(S, 3),
                  pl.BlockSpec(lam_rows.shape, lambda b, h: (0, 0)),
                  pl.BlockSpec((1, LANES), lambda b, h: (0, 0))],
        out_specs=pl.BlockSpec((None, None, S, LANES), lambda b, h: (b, h, 0, 0)),
        out_shape=jax.ShapeDtypeStruct((B, B_HEADS, S, LANES), jnp.bfloat16),
        scratch_shapes=(
            [pltpu.VMEM((S, 2 * LANES), jnp.bfloat16)]
            + [pltpu.VMEM((2 * TQ, (qi + 1) * TQ), jnp.float32) for qi in range(n_q)]
            + [pltpu.VMEM((2 * TQ, (qi + 1) * TQ), jnp.bfloat16) for qi in range(n_q)]),
        compiler_params=_ARB2,
        name="attn_b",
    )(kvqg, kvqg, kvqg, kvqg, lam_rows, subln)


def kernel(x, positions, a_norm_pre, a_w_in, a_rel_bias, a_w_out, a_norm_post, kv_norm, kv_w,
           b_norm_pre, b_w_in, b_lambda_q1, b_lambda_k1, b_lambda_q2, b_lambda_k2, b_subln, b_w_out,
           b_norm_post):
    assert a_w_in.shape[0] == 1 and b_w_in.shape[0] == 1
    bf16 = jnp.bfloat16

    m = np.arange(A_BAND + TQ)
    idx = np.clip(A_BAND - m, -MAX_REL, MAX_REL) + MAX_REL
    brow = jnp.take(a_rel_bias[0], jnp.asarray(idx, jnp.int32), axis=1)

    half = ROT_DIM // 2
    inv_freq = jnp.power(jnp.float32(ROPE_THETA), -jnp.arange(half, dtype=jnp.float32) * 2.0 / ROT_DIM)
    head_freq = jnp.concatenate([inv_freq, inv_freq, jnp.zeros((HEAD_DIM - ROT_DIM,), jnp.float32)])
    freq = jnp.tile(head_freq, LANES // HEAD_DIM)[None, :]

    lam_rows = jnp.zeros((8, LANES), jnp.float32)
    lam_rows = lam_rows.at[0:4, :HEAD_DIM].set(
        jnp.stack([b_lambda_q1[0], b_lambda_k1[0], b_lambda_q2[0], b_lambda_k2[0]]).astype(jnp.float32))

    qkvg = _proj_a(x, a_norm_pre[0][None, :], a_w_in[0].astype(bf16))
    y_a = _attn_a(qkvg, brow)
    tile_pos = positions[:, ::TM, None, None]
    h1, kvqg = _proj_b(y_a, x, a_w_out[0].astype(bf16), a_norm_post[0][None, :], tile_pos, freq,
                       kv_norm[None, :], b_norm_pre[0][None, :], kv_w.astype(bf16), b_w_in[0].astype(bf16))
    y_b = _attn_b(kvqg, lam_rows, b_subln[0][None, :])
    return _out_proj(y_b, h1, b_w_out[0].astype(bf16), b_norm_post[0][None, :], "out_b")
```
